```python
import math
import jax, jax.numpy as jnp
from jax import lax
import numpy as np

D_MODEL = 1024
BATCH = 1
SEQ = 16384
DEPTH = 2
DEC_BATCH = 32
DEC_SEQ = 8
PAST_LEN = 16384
PAGE_SIZE = 128

A_HEADS = 8
A_HEAD_DIM = 64
A_WIDTH = A_HEADS * A_HEAD_DIM
IDX_HEADS = 8
IDX_DIM = 64
TOPK_MAX = 256
Q_BLOCK = 128
N_BUCKETS = 32
MAX_DISTANCE = 128
S5_GROUP = 16
S5_GROUPS = 16
S5_WIDTH = S5_GROUP * S5_GROUPS
S5_STATE = 64
RW_HEADS = 4
RW_HEAD_DIM = 64
RW_WIDTH = RW_HEADS * RW_HEAD_DIM
RW_DECAY_LORA = 64
RW_AAA_LORA = 64
RW_MV_LORA = 32
RW_GATE_LORA = 128
RW_PROJ = 3 * RW_WIDTH + RW_DECAY_LORA + RW_AAA_LORA + RW_GATE_LORA
RW_GN_EPS = 64e-5
A_PROJ = 3 * A_WIDTH + IDX_HEADS * IDX_DIM + IDX_HEADS + IDX_DIM
IN_COLS = A_PROJ + S5_WIDTH + RW_PROJ
N_BRANCH = 3
D_FF = 4 * D_MODEL
PLE_DIM = 256
EPS = 1e-6
NEG = -1e30

kernel_name = 'hybrid_dsa_s5_rwkv7_decoder_step'


def _split_points(sizes):
    pts, acc = [], 0
    for s in sizes:
        acc += s
        pts.append(acc)
    return pts


def rmsnorm(x, g):
    xf = x.astype(jnp.float32)
    y = xf * lax.rsqrt(jnp.mean(xf * xf, axis=-1, keepdims=True) + EPS)
    return (y * g.astype(jnp.float32)).astype(x.dtype)


def t5_bucket(rel):
    n = jnp.maximum(rel, 0)
    max_exact = N_BUCKETS // 2
    nf = jnp.maximum(n, 1).astype(jnp.float32)
    large = max_exact + (jnp.log(nf / max_exact) / math.log(MAX_DISTANCE / max_exact)
                         * (N_BUCKETS - max_exact)).astype(jnp.int32)
    large = jnp.minimum(large, N_BUCKETS - 1)
    return jnp.where(n < max_exact, n, large)


def indexer_scores(q_idx, w_idx, k_idx, q_pos, k_pos):
    s = jnp.einsum('bthd,bsd->bths', q_idx.astype(jnp.float32), k_idx.astype(jnp.float32))
    s = jnp.einsum('bths,bth->bts', jax.nn.relu(s), w_idx.astype(jnp.float32))
    return jnp.where(k_pos[None, :] <= q_pos[:, None], s, NEG)


def sparse_attend(q, k_sel, v_sel, sel_pos, q_pos, rel_bias):
    logits = jnp.einsum('bthd,btkhd->bthk', q, k_sel).astype(jnp.float32) * (A_HEAD_DIM ** -0.5)
    rel = q_pos[None, :, None] - sel_pos
    bias = jnp.swapaxes(rel_bias[t5_bucket(rel)], -1, -2).astype(jnp.float32)
    logits = jnp.where((rel >= 0)[:, :, None, :], logits + bias, NEG)
    p = jax.nn.softmax(logits, axis=-1)
    return jnp.einsum('bthk,btkhd->bthd', p.astype(v_sel.dtype), v_sel)


def dsa_prompt(q, k, v, q_idx, w_idx, k_idx, rel_bias):
    B, T = q.shape[:2]
    topk = min(TOPK_MAX, T // 4)
    n_blk = T // Q_BLOCK
    k_pos = jnp.arange(T, dtype=jnp.int32)
    bi = jnp.arange(B)[:, None, None]

    def to_blocks(a):
        return jnp.moveaxis(a.reshape(B, n_blk, Q_BLOCK, *a.shape[2:]), 1, 0)

    def block(args):
        qb, qib, wb, qpos = args
        sc = indexer_scores(qib, wb, k_idx, qpos, k_pos)
        _, sel = lax.top_k(sc, topk)
        return sparse_attend(qb, k[bi, sel], v[bi, sel], sel, qpos, rel_bias)

    out = lax.map(block, (to_blocks(q), to_blocks(q_idx), to_blocks(w_idx), k_pos.reshape(n_blk, Q_BLOCK)))
    return jnp.moveaxis(out, 0, 1).reshape(B, T, A_WIDTH)


def dsa_sample(layer, q, k_new, v_new, q_idx, w_idx, ki_new, cache_k, cache_v, cache_idx_k, page_table, rel_bias):
    Bd, T = q.shape[:2]
    L = PAST_LEN + T
    topk = min(TOPK_MAX, L // 4)
    ki_past = cache_idx_k[layer, page_table].reshape(Bd, PAST_LEN, IDX_DIM)
    ki_all = jnp.concatenate([ki_past.astype(ki_new.dtype), ki_new], axis=1)
    q_pos = PAST_LEN + jnp.arange(T, dtype=jnp.int32)
    k_pos = jnp.arange(L, dtype=jnp.int32)
    sc = indexer_scores(q_idx, w_idx, ki_all, q_pos, k_pos)
    _, sel = lax.top_k(sc, topk)
    bi = jnp.arange(Bd)[:, None, None]
    past = jnp.minimum(sel, PAST_LEN - 1)
    phys = page_table[bi, past // PAGE_SIZE]
    off = past % PAGE_SIZE
    newi = jnp.clip(sel - PAST_LEN, 0, T - 1)
    is_new = (sel >= PAST_LEN)[..., None, None]
    k_sel = jnp.where(is_new, k_new[bi, newi], cache_k[layer, phys, off].astype(k_new.dtype))
    v_sel = jnp.where(is_new, v_new[bi, newi], cache_v[layer, phys, off].astype(v_new.dtype))
    return sparse_attend(q, k_sel, v_sel, sel, q_pos, rel_bias).reshape(Bd, T, A_WIDTH)


def s5_mixer(u, s0_re, s0_im, lam_re, lam_im, log_step, b_re, b_im, c_re, c_im, d, w_glu):
    f32 = jnp.float32
    B, T = u.shape[:2]
    lam_re = jnp.minimum(lam_re.astype(f32), -1e-4)
    lam_im = lam_im.astype(f32)
    dt = jnp.exp(log_step.astype(f32))[:, None]
    mag = jnp.exp(lam_re * dt)
    ab_re, ab_im = mag * jnp.cos(lam_im * dt), mag * jnp.sin(lam_im * dt)
    den = lam_re * lam_re + lam_im * lam_im
    nr, ni = ab_re - 1.0, ab_im
    f_re = (nr * lam_re + ni * lam_im) / den
    f_im = (ni * lam_re - nr * lam_im) / den
    b_re, b_im = b_re.astype(f32), b_im.astype(f32)
    bb_re = f_re[..., None] * b_re - f_im[..., None] * b_im
    bb_im = f_re[..., None] * b_im + f_im[..., None] * b_re
    ug = u.astype(f32).reshape(B, T, S5_GROUPS, S5_GROUP)
    bu_re = jnp.einsum('btgh,gph->btgp', ug, bb_re)
    bu_im = jnp.einsum('btgh,gph->btgp', ug, bb_im)
    s0_re, s0_im = s0_re.astype(f32), s0_im.astype(f32)
    bu_re = bu_re.at[:, 0].add(ab_re * s0_re - ab_im * s0_im)
    bu_im = bu_im.at[:, 0].add(ab_re * s0_im + ab_im * s0_re)
    a_re = jnp.broadcast_to(ab_re, bu_re.shape)
    a_im = jnp.broadcast_to(ab_im, bu_im.shape)

    def combine(l, r):
        a1r, a1i, b1r, b1i = l
        a2r, a2i, b2r, b2i = r
        return (a2r * a1r - a2i * a1i, a2r * a1i + a2i * a1r,
                a2r * b1r - a2i * b1i + b2r, a2r * b1i + a2i * b1r + b2i)

    _, _, s_re, s_im = lax.associative_scan(combine, (a_re, a_im, bu_re, bu_im), axis=1)
    y = (jnp.einsum('btgp,ghp->btgh', s_re, c_re.astype(f32))
         - jnp.einsum('btgp,ghp->btgh', s_im, c_im.astype(f32)) + ug * d.astype(f32))
    hdn = jax.nn.gelu(y.reshape(B, T, S5_WIDTH))
    out = hdn * jax.nn.sigmoid(hdn @ w_glu.astype(f32))
    return out.astype(u.dtype), s_re[:, -1], s_im[:, -1]


def rwkv_mixer(pc, shift_prev, wkv0, v_first, vres, mu, w0, w2, a0, a2, g2, k_k, k_a, r_k, ln_w, ln_b):
    f32 = jnp.float32
    B, T = pc.shape[:2]
    pcf = pc.astype(f32)
    prev = jnp.concatenate([shift_prev.astype(f32)[:, None], pcf[:, :-1]], axis=1)
    xm = pcf + (prev - pcf) * mu
    pts = _split_points((RW_WIDTH, RW_WIDTH, RW_WIDTH, RW_DECAY_LORA, RW_AAA_LORA))
    r, k, v, wl, al, gl = jnp.split(xm, pts, axis=-1)
    w = -jax.nn.softplus(-(w0 + jnp.tanh(wl) @ w2)) - 0.5
    decay = jnp.exp(-jnp.exp(w))
    a = jax.nn.sigmoid(a0 + al @ a2)
    g = jax.nn.sigmoid(gl) @ g2
    if vres is None:
        v_first = v
    else:
        v0, v1, v2 = vres
        v = v + (v_first - v) * jax.nn.sigmoid(v0 + (v @ v1) @ v2)

    def hs(t):
        return t.reshape(B, T, RW_HEADS, RW_HEAD_DIM)

    kk = hs(k * k_k)
    kk = kk * lax.rsqrt(jnp.maximum(jnp.sum(kk * kk, axis=-1, keepdims=True), 1e-24))
    k = k * (1.0 + (a - 1.0) * k_a)
    r, k, v, decay, a = hs(r), hs(k), hs(v), hs(decay), hs(a)

    def step(S, inp):
        r_t, k_t, v_t, d_t, kk_t, a_t = inp
        sa = jnp.einsum('bhij,bhj->bhi', S, -kk_t)
        S = (S * d_t[:, :, None, :] + sa[..., None] * (kk_t * a_t)[:, :, None, :]
             + v_t[..., None] * k_t[:, :, None, :])
        return S, jnp.einsum('bhij,bhj->bhi', S, r_t)

    xs = tuple(jnp.moveaxis(t, 1, 0) for t in (r, k, v, decay, kk, a))
    S, ys = lax.scan(step, wkv0.astype(f32), xs)
    y = jnp.moveaxis(ys, 0, 1)
    mean = jnp.mean(y, axis=-1, keepdims=True)
    var = jnp.mean(jnp.square(y - mean), axis=-1, keepdims=True)
    y = ((y - mean) * lax.rsqrt(var + RW_GN_EPS)).reshape(B, T, RW_WIDTH) * ln_w + ln_b
    bonus = jnp.sum(r * k * r_k, axis=-1, keepdims=True) * v
    y = (y + bonus.reshape(B, T, RW_WIDTH)) * g
    return y.astype(pc.dtype), S, pc[:, -1], v_first


def trunk(x, ple, attend, s5_re0, s5_im0, wkv0, shift0, W):
    B, T, _ = x.shape
    pts = _split_points((A_WIDTH, A_WIDTH, A_WIDTH, IDX_HEADS * IDX_DIM, IDX_HEADS, IDX_DIM, S5_WIDTH))
    h = x
    v_first = None
    ks, vs, kis, sres, sims, wkvs, shifts = [], [], [], [], [], [], []
    for i in range(DEPTH):
        xn = rmsnorm(h, W['norm_mix'][i])
        proj = xn @ W['w_in'][i]
        q, k, v, qi, wi, ki, u, pc = jnp.split(proj, pts, axis=-1)
        q = q.reshape(B, T, A_HEADS, A_HEAD_DIM)
        k = k.reshape(B, T, A_HEADS, A_HEAD_DIM)
        v = v.reshape(B, T, A_HEADS, A_HEAD_DIM)
        qi = qi.reshape(B, T, IDX_HEADS, IDX_DIM) * (IDX_DIM ** -0.5)
        wi = wi * (IDX_HEADS ** -0.5)
        o_a = attend(i, q, k, v, qi, wi, ki)
        o_b, s_re, s_im = s5_mixer(u, s5_re0[i], s5_im0[i], W['ssm_lambda_re'][i], W['ssm_lambda_im'][i],
                                   W['ssm_log_step'][i], W['ssm_b_re'][i], W['ssm_b_im'][i],
                                   W['ssm_c_re'][i], W['ssm_c_im'][i], W['ssm_d'][i], W['ssm_w_glu'][i])
        vres = None if i == 0 else (W['rw_v0'][i - 1], W['rw_v1'][i - 1], W['rw_v2'][i - 1])
        o_c, wkv, shift, v_first = rwkv_mixer(pc, shift0[i], wkv0[i], v_first, vres, W['rw_mu'][i],
                                              W['rw_w0'][i], W['rw_w2'][i], W['rw_a0'][i], W['rw_a2'][i],
                                              W['rw_g2'][i], W['rw_k_k'][i], W['rw_k_a'][i], W['rw_r_k'][i],
                                              W['rw_ln_w'][i], W['rw_ln_b'][i])
        gates = jax.nn.sigmoid(xn @ W['w_gate'][i]).reshape(B, T, N_BRANCH, D_MODEL)
        mixed = (gates[:, :, 0] * (o_a @ W['w_up_a'][i]) + gates[:, :, 1] * (o_b @ W['w_up_b'][i])
                 + gates[:, :, 2] * (o_c @ W['w_up_c'][i]))
        h = h + mixed @ W['w_out'][i]
        hn = rmsnorm(h, W['norm_ffn'][i])
        h = h + jnp.square(jax.nn.relu(hn @ W['w_ff1'][i])) @ W['w_ff2'][i]
        pg = jax.nn.sigmoid(rmsnorm(h, W['norm_ple'][i]) @ W['w_ple_gate'][i])
        h = h + (ple[i] @ W['w_ple'][i]) * pg
        ks.append(k); vs.append(v); kis.append(ki)
        sres.append(s_re); sims.append(s_im); wkvs.append(wkv); shifts.append(shift)
    y = rmsnorm(h, W['norm_final'])
    return (y, jnp.stack(ks), jnp.stack(vs), jnp.stack(kis), jnp.stack(sres), jnp.stack(sims),
            jnp.stack(wkvs), jnp.stack(shifts))


def setup_inputs(seed: int = 0) -> dict:
    key = jax.random.key(seed)
    keys = iter(jax.random.split(key, 64))
    f32 = jnp.float32

    def nrm(shape, scale=1.0):
        return jax.random.normal(next(keys), shape, f32) * scale

    def unif(shape, lo, hi):
        return jax.random.uniform(next(keys), shape, f32, lo, hi)

    n_pages = PAST_LEN // PAGE_SIZE
    n_used = DEC_BATCH * n_pages
    n_pool = n_used + n_used // 4
    page_table = jax.random.permutation(next(keys), n_pool)[:n_used].reshape(DEC_BATCH, n_pages).astype(jnp.int32)
    lam_im = jnp.pi * jnp.arange(S5_STATE, dtype=f32)
    return {
        'x_prompt': nrm((BATCH, SEQ, D_MODEL)),
        'x_sample': nrm((DEC_BATCH, DEC_SEQ, D_MODEL)),
        'cache_k': nrm((DEPTH, n_pool, PAGE_SIZE, A_HEADS, A_HEAD_DIM)),
        'cache_v': nrm((DEPTH, n_pool, PAGE_SIZE, A_HEADS, A_HEAD_DIM)),
        'cache_idx_k': nrm((DEPTH, n_pool, PAGE_SIZE, IDX_DIM)),
        'state_ssm_re': nrm((DEPTH, DEC_BATCH, S5_GROUPS, S5_STATE), 0.5),
        'state_ssm_im': nrm((DEPTH, DEC_BATCH, S5_GROUPS, S5_STATE), 0.5),
        'state_wkv': nrm((DEPTH, DEC_BATCH, RW_HEADS, RW_HEAD_DIM, RW_HEAD_DIM), 0.5),
        'state_shift': nrm((DEPTH, DEC_BATCH, RW_PROJ)),
        'page_table': page_table,
        'p_prompt': nrm((DEPTH, BATCH, SEQ, PLE_DIM)),
        'p_sample': nrm((DEPTH, DEC_BATCH, DEC_SEQ, PLE_DIM)),
        'rel_bias': nrm((N_BUCKETS, A_HEADS), 0.1),
        'norm_mix': 1.0 + nrm((DEPTH, D_MODEL), 0.02),
        'norm_ffn': 1.0 + nrm((DEPTH, D_MODEL), 0.02),
        'norm_ple': 1.0 + nrm((DEPTH, D_MODEL), 0.02),
        'norm_final': 1.0 + nrm((D_MODEL,), 0.02),
        'w_in': nrm((DEPTH, D_MODEL, IN_COLS), D_MODEL ** -0.5),
        'w_up_a': nrm((DEPTH, A_WIDTH, D_MODEL), A_WIDTH ** -0.5),
        'w_up_b': nrm((DEPTH, S5_WIDTH, D_MODEL), S5_WIDTH ** -0.5),
        'w_up_c': nrm((DEPTH, RW_WIDTH, D_MODEL), RW_WIDTH ** -0.5),
        'w_gate': nrm((DEPTH, D_MODEL, N_BRANCH * D_MODEL), D_MODEL ** -0.5),
        'w_out': nrm((DEPTH, D_MODEL, D_MODEL), D_MODEL ** -0.5),
        'ssm_lambda_re': -0.5 + nrm((DEPTH, S5_GROUPS, S5_STATE), 0.01),
        'ssm_lambda_im': lam_im + nrm((DEPTH, S5_GROUPS, S5_STATE), 0.01),
        'ssm_log_step': unif((DEPTH, S5_GROUPS), math.log(1e-3), math.log(1e-1)),
        'ssm_b_re': nrm((DEPTH, S5_GROUPS, S5_STATE, S5_GROUP), (2 * S5_GROUP) ** -0.5),
        'ssm_b_im': nrm((DEPTH, S5_GROUPS, S5_STATE, S5_GROUP), (2 * S5_GROUP) ** -0.5),
        'ssm_c_re': nrm((DEPTH, S5_GROUPS, S5_GROUP, S5_STATE), (2 * S5_STATE) ** -0.5),
        'ssm_c_im': nrm((DEPTH, S5_GROUPS, S5_GROUP, S5_STATE), (2 * S5_STATE) ** -0.5),
        'ssm_d': nrm((DEPTH, S5_GROUPS, S5_GROUP)),
        'ssm_w_glu': nrm((DEPTH, S5_WIDTH, S5_WIDTH), S5_WIDTH ** -0.5),
        'rw_mu': unif((DEPTH, RW_PROJ), 0.0, 1.0),
        'rw_w0': unif((DEPTH, RW_WIDTH), -6.5, -1.5),
        'rw_w2': nrm((DEPTH, RW_DECAY_LORA, RW_WIDTH), 0.1),
        'rw_a0': nrm((DEPTH, RW_WIDTH), 0.1),
        'rw_a2': nrm((DEPTH, RW_AAA_LORA, RW_WIDTH), 0.1),
        'rw_g2': nrm((DEPTH, RW_GATE_LORA, RW_WIDTH), RW_GATE_LORA ** -0.5),
        'rw_k_k': 0.85 + nrm((DEPTH, RW_WIDTH), 0.05),
        'rw_k_a': 1.0 + nrm((DEPTH, RW_WIDTH), 0.05),
        'rw_r_k': nrm((DEPTH, RW_HEADS, RW_HEAD_DIM), 0.1),
        'rw_ln_w': 1.0 + nrm((DEPTH, RW_WIDTH), 0.02),
        'rw_ln_b': nrm((DEPTH, RW_WIDTH), 0.02),
        'rw_v0': 1.0 + nrm((DEPTH - 1, RW_WIDTH), 0.1),
        'rw_v1': nrm((DEPTH - 1, RW_WIDTH, RW_MV_LORA), RW_WIDTH ** -0.5),
        'rw_v2': nrm((DEPTH - 1, RW_MV_LORA, RW_WIDTH), 0.1),
        'w_ff1': nrm((DEPTH, D_MODEL, D_FF), D_MODEL ** -0.5),
        'w_ff2': nrm((DEPTH, D_FF, D_MODEL), 0.5 * D_FF ** -0.5),
        'w_ple': nrm((DEPTH, PLE_DIM, D_MODEL), PLE_DIM ** -0.5),
        'w_ple_gate': nrm((DEPTH, D_MODEL, D_MODEL), D_MODEL ** -0.5),
    }


def reference(x_prompt, x_sample, cache_k, cache_v, cache_idx_k, state_ssm_re, state_ssm_im, state_wkv,
              state_shift, page_table, p_prompt, p_sample, rel_bias, norm_mix, norm_ffn, norm_ple, norm_final,
              w_in, w_up_a, w_up_b, w_up_c, w_gate, w_out, ssm_lambda_re, ssm_lambda_im, ssm_log_step,
              ssm_b_re, ssm_b_im, ssm_c_re, ssm_c_im, ssm_d, ssm_w_glu, rw_mu, rw_w0, rw_w2, rw_a0, rw_a2,
              rw_g2, rw_k_k, rw_k_a, rw_r_k, rw_ln_w, rw_ln_b, rw_v0, rw_v1, rw_v2, w_ff1, w_ff2,
              w_ple, w_ple_gate):
    W = dict(norm_mix=norm_mix, norm_ffn=norm_ffn, norm_ple=norm_ple, norm_final=norm_final, w_in=w_in,
             w_up_a=w_up_a, w_up_b=w_up_b, w_up_c=w_up_c, w_gate=w_gate, w_out=w_out,
             ssm_lambda_re=ssm_lambda_re, ssm_lambda_im=ssm_lambda_im, ssm_log_step=ssm_log_step,
             ssm_b_re=ssm_b_re, ssm_b_im=ssm_b_im, ssm_c_re=ssm_c_re, ssm_c_im=ssm_c_im, ssm_d=ssm_d,
             ssm_w_glu=ssm_w_glu, rw_mu=rw_mu, rw_w0=rw_w0, rw_w2=rw_w2, rw_a0=rw_a0, rw_a2=rw_a2,
             rw_g2=rw_g2, rw_k_k=rw_k_k, rw_k_a=rw_k_a, rw_r_k=rw_r_k, rw_ln_w=rw_ln_w, rw_ln_b=rw_ln_b,
             rw_v0=rw_v0, rw_v1=rw_v1, rw_v2=rw_v2, w_ff1=w_ff1, w_ff2=w_ff2, w_ple=w_ple,
             w_ple_gate=w_ple_gate)

    def attend_prompt(i, q, k, v, qi, wi, ki):
        return dsa_prompt(q, k, v, qi, wi, ki, rel_bias)

    def attend_sample(i, q, k, v, qi, wi, ki):
        return dsa_sample(i, q, k, v, qi, wi, ki, cache_k, cache_v, cache_idx_k, page_table, rel_bias)

    zs = jnp.zeros((DEPTH, BATCH, S5_GROUPS, S5_STATE), jnp.float32)
    zw = jnp.zeros((DEPTH, BATCH, RW_HEADS, RW_HEAD_DIM, RW_HEAD_DIM), jnp.float32)
    zsh = jnp.zeros((DEPTH, BATCH, RW_PROJ), x_prompt.dtype)
    (y_prompt, k_prompt, v_prompt, idx_k_prompt, ssm_re_prompt, ssm_im_prompt, wkv_prompt,
     shift_prompt) = trunk(x_prompt, p_prompt, attend_prompt, zs, zs, zw, zsh, W)
    (y_sample, k_sample, v_sample, idx_k_sample, ssm_re_sample, ssm_im_sample, wkv_sample,
     shift_sample) = trunk(x_sample, p_sample, attend_sample, state_ssm_re, state_ssm_im, state_wkv,
                           state_shift, W)
    return (y_prompt, y_sample, k_prompt, v_prompt, idx_k_prompt, k_sample, v_sample, idx_k_sample,
            ssm_re_prompt, ssm_im_prompt, ssm_re_sample, ssm_im_sample, wkv_prompt, wkv_sample,
            shift_prompt, shift_sample)
```

```python
import functools
import math

import numpy as np
import jax
import jax.numpy as jnp
from jax import lax
from jax.experimental import pallas as pl
from jax.experimental.pallas import tpu as pltpu

F32 = jnp.float32
BF16 = jnp.bfloat16
I32 = jnp.int32

D_MODEL = 1024
A_HEADS = 8
A_HEAD_DIM = 64
A_WIDTH = A_HEADS * A_HEAD_DIM
IDX_HEADS = 8
IDX_DIM = 64
TOPK_MAX = 256
N_BUCKETS = 32
MAX_DISTANCE = 128
S5_GROUP = 16
S5_GROUPS = 16
S5_WIDTH = 256
S5_STATE = 64
RW_HEADS = 4
RW_HEAD_DIM = 64
RW_WIDTH = 256
RW_DECAY_LORA = 64
RW_AAA_LORA = 64
RW_GATE_LORA = 128
RW_PROJ = 1024
RW_GN_EPS = 64e-5
D_FF = 4 * D_MODEL
EPS = 1e-6
NEG = -1e30
INT_MIN = -(2 ** 31)

LANES = 128
SUBLANES = 8
VMEM_LIMIT = 48 * 1024 * 1024

_PIECES = (("q", A_WIDTH), ("k", A_WIDTH), ("v", A_WIDTH), ("qi", IDX_HEADS * IDX_DIM),
           ("wi", IDX_HEADS), ("ki", IDX_DIM), ("u", S5_WIDTH), ("pc", RW_PROJ))


def _piece_layout():
    src, dst, out = 0, 0, {}
    for name, width in _PIECES:
        padded = -(-width // LANES) * LANES
        out[name] = (src, dst, width)
        src += width
        dst += padded
    return out, dst


_LAYOUT, PROJ_COLS = _piece_layout()

DSA_TQ = 128
DSA_TK = 512


def _cparams(sem):
    return pltpu.CompilerParams(dimension_semantics=sem, vmem_limit_bytes=VMEM_LIMIT)


def _rmsnorm(x, g):
    return x * lax.rsqrt(jnp.mean(x * x, axis=-1, keepdims=True) + EPS) * g


def _dot_exact(a, b):
    return jnp.dot(a, b, precision=lax.Precision.HIGHEST, preferred_element_type=F32)


def _dot_nt(a, b):
    return lax.dot_general(a, b, (((1,), (1,)), ((), ())), preferred_element_type=F32)


def _row_tile(m):
    return 512 if m % 512 == 0 else 256


def _norm_mm_kernel(x_ref, g_ref, w_ref, s_ref, o_ref, xn_ref, *, act):
    @pl.when(pl.program_id(1) == 0)
    def _():
        xn_ref[...] = _rmsnorm(x_ref[...], g_ref[...]).astype(BF16)

    y = jnp.dot(xn_ref[...], w_ref[...], preferred_element_type=F32)
    if act == "sigmoid":
        y = jax.nn.sigmoid(y)
    o_ref[...] = y * s_ref[...]


def _norm_mm(x, g, w_bf16, scale_row, act=None, tn=512):
    m, d = x.shape
    n = w_bf16.shape[1]
    tm = _row_tile(m)
    return pl.pallas_call(
        functools.partial(_norm_mm_kernel, act=act),
        grid=(m // tm, n // tn),
        in_specs=[pl.BlockSpec((tm, d), lambda i, j: (i, 0)),
                  pl.BlockSpec((1, d), lambda i, j: (0, 0)),
                  pl.BlockSpec((d, tn), lambda i, j: (0, j)),
                  pl.BlockSpec((1, tn), lambda i, j: (0, j))],
        out_specs=pl.BlockSpec((tm, tn), lambda i, j: (i, j)),
        out_shape=jax.ShapeDtypeStruct((m, n), F32),
        scratch_shapes=[pltpu.VMEM((tm, d), BF16)],
        compiler_params=_cparams(("parallel", "arbitrary")),
        name="norm_mm",
    )(x, g.reshape(1, d), w_bf16, scale_row)


def _merge_kernel(oa_ref, ob_ref, oc_ref, g_ref, h_ref, wa_ref, wb_ref, wc_ref, wo_ref, o_ref):
    g = g_ref[...]
    ya = jnp.dot(oa_ref[...].astype(BF16), wa_ref[...], preferred_element_type=F32)
    yb = jnp.dot(ob_ref[...].astype(BF16), wb_ref[...], preferred_element_type=F32)
    yc = jnp.dot(oc_ref[...].astype(BF16), wc_ref[...], preferred_element_type=F32)
    mixed = (g[:, 0:D_MODEL] * ya + g[:, D_MODEL:2 * D_MODEL] * yb
             + g[:, 2 * D_MODEL:3 * D_MODEL] * yc)
    o_ref[...] = h_ref[...] + jnp.dot(mixed.astype(BF16), wo_ref[...], preferred_element_type=F32)


def _merge(oa, ob, oc, gates, h, wa, wb, wc, wo):
    m = h.shape[0]
    tm = 256
    row = lambda w: pl.BlockSpec((tm, w), lambda i: (i, 0))
    full = lambda a: pl.BlockSpec(a.shape, lambda i: (0, 0))
    return pl.pallas_call(
        _merge_kernel,
        grid=(m // tm,),
        in_specs=[row(A_WIDTH), row(S5_WIDTH), row(RW_WIDTH), row(3 * D_MODEL), row(D_MODEL),
                  full(wa), full(wb), full(wc), full(wo)],
        out_specs=row(D_MODEL),
        out_shape=jax.ShapeDtypeStruct((m, D_MODEL), F32),
        compiler_params=_cparams(("parallel",)),
        name="merge_out",
    )(oa, ob, oc, gates, h, wa, wb, wc, wo)


def _ffn_kernel(h_ref, g_ref, w1_ref, w2_ref, o_ref, hn_ref):
    @pl.when(pl.program_id(1) == 0)
    def _():
        h = h_ref[...]
        hn_ref[...] = _rmsnorm(h, g_ref[...]).astype(BF16)
        o_ref[...] = h

    a = jnp.dot(hn_ref[...], w1_ref[...], preferred_element_type=F32)
    a = jnp.square(jnp.maximum(a, 0.0))
    o_ref[...] += jnp.dot(a.astype(BF16), w2_ref[...], preferred_element_type=F32)


def _ffn(h, g, w1, w2, tf=512):
    m, d = h.shape
    f = w1.shape[1]
    tm = _row_tile(m)
    return pl.pallas_call(
        _ffn_kernel,
        grid=(m // tm, f // tf),
        in_specs=[pl.BlockSpec((tm, d), lambda i, j: (i, 0)),
                  pl.BlockSpec((1, d), lambda i, j: (0, 0)),
                  pl.BlockSpec((d, tf), lambda i, j: (0, j)),
                  pl.BlockSpec((tf, d), lambda i, j: (j, 0))],
        out_specs=pl.BlockSpec((tm, d), lambda i, j: (i, 0)),
        out_shape=jax.ShapeDtypeStruct((m, d), F32),
        scratch_shapes=[pltpu.VMEM((tm, d), BF16)],
        compiler_params=_cparams(("parallel", "arbitrary")),
        name="ffn",
    )(h, g.reshape(1, d), w1, w2)


def _ple_kernel(h_ref, p_ref, g_ref, gf_ref, wg_ref, wp_ref, o_ref, y_ref):
    h = h_ref[...]
    pg = jax.nn.sigmoid(jnp.dot(_rmsnorm(h, g_ref[...]).astype(BF16), wg_ref[...],
                                preferred_element_type=F32))
    e = jnp.dot(p_ref[...].astype(BF16), wp_ref[...], preferred_element_type=F32)
    out = h + e * pg
    o_ref[...] = out
    y_ref[...] = _rmsnorm(out, gf_ref[...])


def _ple(h, ple, g, g_final, wg, wp):
    m, d = h.shape
    tm = 256
    row = lambda w: pl.BlockSpec((tm, w), lambda i: (i, 0))
    full = lambda shape: pl.BlockSpec(shape, lambda i: (0, 0))
    return pl.pallas_call(
        _ple_kernel,
        grid=(m // tm,),
        in_specs=[row(d), row(ple.shape[1]), full((1, d)), full((1, d)), full(wg.shape), full(wp.shape)],
        out_specs=[row(d), row(d)],
        out_shape=[jax.ShapeDtypeStruct((m, d), F32), jax.ShapeDtypeStruct((m, d), F32)],
        compiler_params=_cparams(("parallel",)),
        name="ple",
    )(h, ple, g.reshape(1, d), g_final.reshape(1, d), wg, wp)


def _sortable_key(s):
    b = lax.bitcast_convert_type(s, I32)
    return b ^ ((b >> 31) & 0x7FFFFFFF)


def _kth_largest_key(count_ge, rows, k):
    zero = jnp.zeros((rows, 1), I32)
    thr = jnp.where(count_ge(zero) >= k, zero, jnp.full((rows, 1), INT_MIN, I32))

    def bit_step(it, thr):
        cand = thr + jnp.left_shift(jnp.int32(1), 30 - it)
        return jnp.where(count_ge(cand) >= k, cand, thr)

    return lax.fori_loop(0, 31, bit_step, thr)


def _t5_bucket(rel):
    n = jnp.maximum(rel, 0)
    max_exact = N_BUCKETS // 2
    nf = jnp.maximum(n, 1).astype(F32)
    large = max_exact + (jnp.log(nf / max_exact) / math.log(MAX_DISTANCE / max_exact)
                         * (N_BUCKETS - max_exact)).astype(I32)
    large = jnp.minimum(large, N_BUCKETS - 1)
    return jnp.where(n < max_exact, n, large)


def _bias_of_rel(rel_bias, rel):
    b = jnp.moveaxis(rel_bias[_t5_bucket(rel)], -1, 0).astype(F32)
    return jnp.where((rel >= 0)[None], b, NEG)


def _dsa_prompt_kernel(qb_ref, kb_ref, tid_ref, q_ref, k_ref, v_ref, qi_ref, wi_ref, ki_ref,
                       d_ref, o_ref, keys_ref, thr_ref, m_ref, l_ref, acc_ref, *, topk, ratio):
    p = pl.program_id(0)
    i = qb_ref[p]
    j = kb_ref[p]
    tq, tk = DSA_TQ, DSA_TK
    jd = i // ratio

    @pl.when(j == 0)
    def _select():
        w = wi_ref[...]
        qpos = i * tq + lax.broadcasted_iota(I32, (tq, tk), 0)
        lane = lax.broadcasted_iota(I32, (tq, tk), 1)

        def chunk(c, carry):
            off = pl.multiple_of(c * tk, tk)
            kic = ki_ref[pl.ds(off, tk), :]
            s = jnp.zeros((tq, tk), F32)
            for h in range(IDX_HEADS):
                s = s + jnp.maximum(_dot_nt(qi_ref[h], kic), 0.0) * w[:, h:h + 1]
            s = jnp.where(off + lane <= qpos, s, NEG)
            keys_ref[c] = _sortable_key(s)
            return carry

        lax.fori_loop(0, jd + 1, chunk, 0)

        def count_ge(cand):
            def body(c, acc):
                ge = (keys_ref[c] >= cand).astype(I32)
                for t in range(tk // LANES):
                    acc = acc + ge[:, t * LANES:(t + 1) * LANES]
                return acc
            acc = lax.fori_loop(0, jd + 1, body, jnp.zeros((tq, LANES), I32))
            return jnp.sum(acc, axis=1, keepdims=True)

        thr_ref[...] = _kth_largest_key(count_ge, tq, topk)
        m_ref[...] = jnp.full(m_ref.shape, NEG, F32)
        l_ref[...] = jnp.zeros(l_ref.shape, F32)
        acc_ref[...] = jnp.zeros(acc_ref.shape, F32)

    sel = keys_ref[j] >= thr_ref[...]
    for h in range(A_HEADS):
        lg = _dot_nt(q_ref[h], k_ref[h]) * (A_HEAD_DIM ** -0.5) + d_ref[0, h]
        lg = jnp.where(sel, lg, NEG)
        m_prev = m_ref[h]
        m_new = jnp.maximum(m_prev, jnp.max(lg, axis=1, keepdims=True))
        alpha = jnp.exp(m_prev - m_new)
        pr = jnp.exp(lg - m_new)
        l_ref[h] = alpha * l_ref[h] + jnp.sum(pr, axis=1, keepdims=True)
        acc_ref[h] = alpha * acc_ref[h] + jnp.dot(pr.astype(BF16), v_ref[h],
                                                  preferred_element_type=F32)
        m_ref[h] = m_new

    @pl.when(j == jd)
    def _finish():
        for h in range(A_HEADS):
            o_ref[h] = acc_ref[h] / l_ref[h]


def _dsa_prompt(q, k, v, qi, wi, ki, rel_bias):
    t = q.shape[0]
    tq, tk = DSA_TQ, DSA_TK
    ratio = tk // tq
    topk = min(TOPK_MAX, t // 4)
    n_q = t // tq
    n_kc = t // tk

    qb, kb, tid = [], [], []
    for i in range(n_q):
        jd, a = divmod(i, ratio)
        for j in range(jd + 1):
            qb.append(i)
            kb.append(j)
            tid.append(a if j == jd else (ratio if (j == jd - 1 and a == 0) else ratio + 1))
    qb, kb, tid = (jnp.asarray(np.asarray(x, np.int32)) for x in (qb, kb, tid))

    r = jnp.arange(tq, dtype=I32)[:, None]
    c = jnp.arange(tk, dtype=I32)[None, :]
    tiles = [_bias_of_rel(rel_bias, a * tq + r - c) for a in range(ratio)]
    tiles.append(_bias_of_rel(rel_bias, tk + r - c))
    tiles.append(_bias_of_rel(rel_bias, jnp.full((tq, tk), MAX_DISTANCE, I32)))
    dtiles = jnp.stack(tiles)

    def heads(x):
        return x.reshape(t, -1, A_HEAD_DIM).transpose(1, 0, 2).astype(BF16)

    grid_spec = pltpu.PrefetchScalarGridSpec(
        num_scalar_prefetch=3,
        grid=(int(qb.shape[0]),),
        in_specs=[
            pl.BlockSpec((A_HEADS, tq, A_HEAD_DIM), lambda p, qb, kb, tid: (0, qb[p], 0)),
            pl.BlockSpec((A_HEADS, tk, A_HEAD_DIM), lambda p, qb, kb, tid: (0, kb[p], 0)),
            pl.BlockSpec((A_HEADS, tk, A_HEAD_DIM), lambda p, qb, kb, tid: (0, kb[p], 0)),
            pl.BlockSpec((IDX_HEADS, tq, IDX_DIM), lambda p, qb, kb, tid: (0, qb[p], 0)),
            pl.BlockSpec((tq, IDX_HEADS), lambda p, qb, kb, tid: (qb[p], 0)),
            pl.BlockSpec((t, IDX_DIM), lambda p, qb, kb, tid: (0, 0)),
            pl.BlockSpec((1, A_HEADS, tq, tk), lambda p, qb, kb, tid: (tid[p], 0, 0, 0)),
        ],
        out_specs=pl.BlockSpec((A_HEADS, tq, A_HEAD_DIM), lambda p, qb, kb, tid: (0, qb[p], 0)),
        scratch_shapes=[pltpu.VMEM((n_kc, tq, tk), I32),
                        pltpu.VMEM((tq, 1), I32),
                        pltpu.VMEM((A_HEADS, tq, 1), F32),
                        pltpu.VMEM((A_HEADS, tq, 1), F32),
                        pltpu.VMEM((A_HEADS, tq, A_HEAD_DIM), F32)],
    )
    out = pl.pallas_call(
        functools.partial(_dsa_prompt_kernel, topk=topk, ratio=ratio),
        grid_spec=grid_spec,
        out_shape=jax.ShapeDtypeStruct((A_HEADS, t, A_HEAD_DIM), F32),
        compiler_params=_cparams(("arbitrary",)),
        name="dsa_prompt",
    )(qb, kb, tid, heads(q), heads(k), heads(v), heads(qi), wi, ki.astype(BF16), dtiles)
    return out.transpose(1, 0, 2).reshape(t, A_WIDTH)


def _dsa_sample_select_kernel(pt_ref, qi_ref, wi_ref, cki_ref, kin_ref, keys_ref, thr_ref,
                              *, n_pages, n_new, topk):
    p = pl.program_id(1)
    rows = n_new

    def scores(kblk):
        r = jnp.maximum(_dot_nt(qi_ref[0], kblk.astype(BF16)), 0.0) * wi_ref[0]
        return jnp.sum(r.reshape(IDX_HEADS, rows, r.shape[-1]), axis=0)

    @pl.when(p < n_pages)
    def _past():
        keys_ref[0, p] = _sortable_key(scores(cki_ref[0, 0]))

    @pl.when(p == n_pages)
    def _new():
        s = scores(kin_ref[0])
        qrow = lax.broadcasted_iota(I32, s.shape, 0)
        col = lax.broadcasted_iota(I32, s.shape, 1)
        keys_ref[0, p] = _sortable_key(jnp.where(col <= qrow, s, NEG))

        def count_ge(cand):
            def body(c, acc):
                return acc + (keys_ref[0, c] >= cand).astype(I32)
            acc = lax.fori_loop(0, n_pages + 1, body, jnp.zeros((rows, LANES), I32))
            return jnp.sum(acc, axis=1, keepdims=True)

        thr = _kth_largest_key(count_ge, rows, topk)
        thr_ref[0] = jnp.broadcast_to(thr, (rows, LANES))


def _dsa_sample_attend_kernel(pt_ref, q_ref, keys_ref, thr_ref, ck_ref, cv_ref, kn_ref, vn_ref,
                              d_ref, rep_ref, o_ref, m_ref, l_ref, acc_ref, *, n_pages):
    p = pl.program_id(1)

    @pl.when(p == 0)
    def _init():
        m_ref[...] = jnp.full(m_ref.shape, NEG, F32)
        l_ref[...] = jnp.zeros(l_ref.shape, F32)
        acc_ref[...] = jnp.zeros(acc_ref.shape, F32)

    def attend(k2, v2):
        sel = (keys_ref[0, 0] >= thr_ref[0]).astype(BF16)
        selx = jnp.dot(sel, rep_ref[...], preferred_element_type=F32)
        selx = jnp.concatenate([selx] * A_HEADS, axis=0)
        lg = _dot_nt(q_ref[0], k2.astype(BF16)) * (A_HEAD_DIM ** -0.5) + d_ref[0]
        lg = jnp.where(selx > 0.5, lg, NEG)
        m_prev = m_ref[...]
        m_new = jnp.maximum(m_prev, jnp.max(lg, axis=1, keepdims=True))
        alpha = jnp.exp(m_prev - m_new)
        pr = jnp.exp(lg - m_new)
        l_ref[...] = alpha * l_ref[...] + jnp.sum(pr, axis=1, keepdims=True)
        acc_ref[...] = alpha * acc_ref[...] + jnp.dot(pr.astype(BF16), v2.astype(BF16),
                                                      preferred_element_type=F32)
        m_ref[...] = m_new

    @pl.when(p < n_pages)
    def _past():
        attend(ck_ref[0, 0], cv_ref[0, 0])

    @pl.when(p == n_pages)
    def _new():
        attend(kn_ref[0], vn_ref[0])
        o_ref[0] = acc_ref[...] / l_ref[...]


def _dsa_sample(layer, q, k_new, v_new, qi, wi, ki_new, cache_k, cache_v, cache_idx_k,
                page_table, rel_bias):
    bd, n_pages = page_table.shape
    page = cache_k.shape[2]
    n_new = q.shape[0] // bd
    past = n_pages * page
    topk = min(TOPK_MAX, (past + n_new) // 4)
    hrows = A_HEADS * n_new
    pcols = page * A_HEADS
    pt = page_table.reshape(-1).astype(I32)
    n_pool = cache_k.shape[1]

    def head_rows(x):
        return x.reshape(bd, n_new, -1, A_HEAD_DIM).transpose(0, 2, 1, 3).reshape(bd, hrows, A_HEAD_DIM)

    qi2 = head_rows(qi).astype(BF16)
    wi2 = jnp.broadcast_to(wi.reshape(bd, n_new, IDX_HEADS).transpose(0, 2, 1).reshape(bd, hrows, 1),
                           (bd, hrows, LANES))
    kin = jnp.pad(ki_new.reshape(bd, n_new, IDX_DIM), ((0, 0), (0, page - n_new), (0, 0)))

    def page_idx(b, p, pt):
        return pt[b * n_pages + jnp.minimum(p, n_pages - 1)]

    keys, thr = pl.pallas_call(
        functools.partial(_dsa_sample_select_kernel, n_pages=n_pages, n_new=n_new, topk=topk),
        grid_spec=pltpu.PrefetchScalarGridSpec(
            num_scalar_prefetch=1,
            grid=(bd, n_pages + 1),
            in_specs=[
                pl.BlockSpec((1, hrows, IDX_DIM), lambda b, p, pt: (b, 0, 0)),
                pl.BlockSpec((1, hrows, LANES), lambda b, p, pt: (b, 0, 0)),
                pl.BlockSpec((1, 1, page, IDX_DIM), lambda b, p, pt: (layer, page_idx(b, p, pt), 0, 0)),
                pl.BlockSpec((1, page, IDX_DIM), lambda b, p, pt: (b, 0, 0)),
            ],
            out_specs=[pl.BlockSpec((1, n_pages + 1, n_new, page), lambda b, p, pt: (b, 0, 0, 0)),
                       pl.BlockSpec((1, n_new, LANES), lambda b, p, pt: (b, 0, 0))],
        ),
        out_shape=[jax.ShapeDtypeStruct((bd, n_pages + 1, n_new, page), I32),
                   jax.ShapeDtypeStruct((bd, n_new, LANES), I32)],
        compiler_params=_cparams(("parallel", "arbitrary")),
        name="dsa_sample_select",
    )(pt, qi2, wi2, cache_idx_k, kin)

    hq = jnp.arange(hrows, dtype=I32)
    ph = jnp.arange(pcols, dtype=I32)
    same_head = (hq[:, None] // n_new) == (ph[None, :] % A_HEADS)
    qq = (hq % n_new)[:, None]
    pos = (ph // A_HEADS)[None, :]

    def tile(rel, ok):
        b = _bias_of_rel(rel_bias, rel)
        b = b[(hq // n_new)[:, None], hq[:, None], ph[None, :]]
        return jnp.where(same_head & ok, b, NEG)

    always = jnp.ones((hrows, pcols), bool)
    dtiles = jnp.stack([
        tile(jnp.full((hrows, pcols), MAX_DISTANCE, I32), always),
        tile(page + qq - pos, always),
        tile(qq - pos, pos < n_new),
    ])
    rep = (jnp.arange(page, dtype=I32)[:, None] == pos).astype(BF16)

    def pos_rows(x):
        x = x.reshape(bd, n_new * A_HEADS, A_HEAD_DIM)
        return jnp.pad(x, ((0, 0), (0, pcols - n_new * A_HEADS), (0, 0)))

    def tile_idx(p):
        return jnp.where(p == n_pages, 2, jnp.where(p == n_pages - 1, 1, 0))

    ck = cache_k.reshape(cache_k.shape[0], n_pool, pcols, A_HEAD_DIM)
    cv = cache_v.reshape(cache_v.shape[0], n_pool, pcols, A_HEAD_DIM)
    out = pl.pallas_call(
        functools.partial(_dsa_sample_attend_kernel, n_pages=n_pages),
        grid_spec=pltpu.PrefetchScalarGridSpec(
            num_scalar_prefetch=1,
            grid=(bd, n_pages + 1),
            in_specs=[
                pl.BlockSpec((1, hrows, A_HEAD_DIM), lambda b, p, pt: (b, 0, 0)),
                pl.BlockSpec((1, 1, n_new, page), lambda b, p, pt: (b, p, 0, 0)),
                pl.BlockSpec((1, n_new, LANES), lambda b, p, pt: (b, 0, 0)),
                pl.BlockSpec((1, 1, pcols, A_HEAD_DIM), lambda b, p, pt: (layer, page_idx(b, p, pt), 0, 0)),
                pl.BlockSpec((1, 1, pcols, A_HEAD_DIM), lambda b, p, pt: (layer, page_idx(b, p, pt), 0, 0)),
                pl.BlockSpec((1, pcols, A_HEAD_DIM), lambda b, p, pt: (b, 0, 0)),
                pl.BlockSpec((1, pcols, A_HEAD_DIM), lambda b, p, pt: (b, 0, 0)),
                pl.BlockSpec((1, hrows, pcols), lambda b, p, pt: (tile_idx(p), 0, 0)),
                pl.BlockSpec((page, pcols), lambda b, p, pt: (0, 0)),
            ],
            out_specs=pl.BlockSpec((1, hrows, A_HEAD_DIM), lambda b, p, pt: (b, 0, 0)),
            scratch_shapes=[pltpu.VMEM((hrows, 1), F32), pltpu.VMEM((hrows, 1), F32),
                            pltpu.VMEM((hrows, A_HEAD_DIM), F32)],
        ),
        out_shape=jax.ShapeDtypeStruct((bd, hrows, A_HEAD_DIM), F32),
        compiler_params=_cparams(("parallel", "arbitrary")),
        name="dsa_sample_attend",
    )(pt, head_rows(q).astype(BF16), keys, thr, ck, cv, pos_rows(k_new), pos_rows(v_new), dtiles, rep)
    return out.reshape(bd, A_HEADS, n_new, A_HEAD_DIM).transpose(0, 2, 1, 3).reshape(bd * n_new, A_WIDTH)


def _s5_in_kernel(u_ref, bb_ref, o_ref):
    o_ref[...] = _dot_exact(u_ref[...], bb_ref[...])


def _s5_scan_kernel(bu_ref, s0_ref, ab_ref, o_ref, fin_ref, st_ref, *, tc):
    @pl.when(pl.program_id(1) == 0)
    def _():
        st_ref[...] = s0_ref[0]

    ar, ai = ab_ref[0], ab_ref[1]

    def step(t, carry):
        sr, si = carry
        nr = ar * sr - ai * si + bu_ref[0, t, 0]
        ni = ar * si + ai * sr + bu_ref[0, t, 1]
        o_ref[0, t, 0] = nr
        o_ref[0, t, 1] = ni
        return nr, ni

    sr, si = lax.fori_loop(0, tc, step, (st_ref[0], st_ref[1]), unroll=8)
    st_ref[0] = sr
    st_ref[1] = si
    fin_ref[0, 0] = sr
    fin_ref[0, 1] = si


def _s5_out_kernel(s_ref, u_ref, c_ref, d_ref, wg_ref, o_ref):
    u = u_ref[...]
    y = _dot_exact(s_ref[...], c_ref[...]) + u * d_ref[...]
    hdn = jax.nn.gelu(y)
    gate = jax.nn.sigmoid(jnp.dot(hdn.astype(BF16), wg_ref[...], preferred_element_type=F32))
    o_ref[...] = hdn * gate


def _s5(u, s0_re, s0_im, lam_re, lam_im, log_step, b_re, b_im, c_re, c_im, d, w_glu, batch):
    m = u.shape[0]
    t = m // batch
    g, p = S5_GROUPS, S5_STATE
    n_state = g * p
    lam_re = jnp.minimum(lam_re.astype(F32), -1e-4)
    lam_im = lam_im.astype(F32)
    dt = jnp.exp(log_step.astype(F32))[:, None]
    mag = jnp.exp(lam_re * dt)
    ab_re, ab_im = mag * jnp.cos(lam_im * dt), mag * jnp.sin(lam_im * dt)
    den = lam_re * lam_re + lam_im * lam_im
    nr, ni = ab_re - 1.0, ab_im
    f_re = (nr * lam_re + ni * lam_im) / den
    f_im = (ni * lam_re - nr * lam_im) / den
    bb_re = f_re[..., None] * b_re - f_im[..., None] * b_im
    bb_im = f_re[..., None] * b_im + f_im[..., None] * b_re
    eye = jnp.eye(g, dtype=F32)
    blk_in = lambda x: jnp.einsum("gph,gk->ghkp", x, eye).reshape(S5_WIDTH, n_state)
    blk_out = lambda x: jnp.einsum("ghp,gk->kpgh", x, eye).reshape(n_state, S5_WIDTH)
    bb = jnp.concatenate([blk_in(bb_re), blk_in(bb_im)], axis=1)
    cc = jnp.concatenate([blk_out(c_re.astype(F32)), -blk_out(c_im.astype(F32))], axis=0)
    ab = jnp.stack([ab_re.reshape(SUBLANES, LANES), ab_im.reshape(SUBLANES, LANES)])
    s0 = jnp.stack([s0_re.reshape(batch, SUBLANES, LANES), s0_im.reshape(batch, SUBLANES, LANES)], axis=1)

    tm = 256
    bu = pl.pallas_call(
        _s5_in_kernel,
        grid=(m // tm,),
        in_specs=[pl.BlockSpec((tm, S5_WIDTH), lambda i: (i, 0)),
                  pl.BlockSpec(bb.shape, lambda i: (0, 0))],
        out_specs=pl.BlockSpec((tm, 2 * n_state), lambda i: (i, 0)),
        out_shape=jax.ShapeDtypeStruct((m, 2 * n_state), F32),
        compiler_params=_cparams(("parallel",)),
        name="s5_in",
    )(u, bb)

    tc = min(t, 256)
    tile = (1, tc, 2, SUBLANES, LANES)
    states, fin = pl.pallas_call(
        functools.partial(_s5_scan_kernel, tc=tc),
        grid=(batch, t // tc),
        in_specs=[pl.BlockSpec(tile, lambda b, c: (b, c, 0, 0, 0)),
                  pl.BlockSpec((1, 2, SUBLANES, LANES), lambda b, c: (b, 0, 0, 0)),
                  pl.BlockSpec((2, SUBLANES, LANES), lambda b, c: (0, 0, 0))],
        out_specs=[pl.BlockSpec(tile, lambda b, c: (b, c, 0, 0, 0)),
                   pl.BlockSpec((1, 2, SUBLANES, LANES), lambda b, c: (b, 0, 0, 0))],
        out_shape=[jax.ShapeDtypeStruct((batch, t, 2, SUBLANES, LANES), F32),
                   jax.ShapeDtypeStruct((batch, 2, SUBLANES, LANES), F32)],
        scratch_shapes=[pltpu.VMEM((2, SUBLANES, LANES), F32)],
        compiler_params=_cparams(("parallel", "arbitrary")),
        name="s5_scan",
    )(bu.reshape(batch, t, 2, SUBLANES, LANES), s0, ab)

    out = pl.pallas_call(
        _s5_out_kernel,
        grid=(m // tm,),
        in_specs=[pl.BlockSpec((tm, 2 * n_state), lambda i: (i, 0)),
                  pl.BlockSpec((tm, S5_WIDTH), lambda i: (i, 0)),
                  pl.BlockSpec(cc.shape, lambda i: (0, 0)),
                  pl.BlockSpec((1, S5_WIDTH), lambda i: (0, 0)),
                  pl.BlockSpec((S5_WIDTH, S5_WIDTH), lambda i: (0, 0))],
        out_specs=pl.BlockSpec((tm, S5_WIDTH), lambda i: (i, 0)),
        out_shape=jax.ShapeDtypeStruct((m, S5_WIDTH), F32),
        compiler_params=_cparams(("parallel",)),
        name="s5_out",
    )(states.reshape(m, 2 * n_state), u, cc, d.reshape(1, S5_WIDTH).astype(F32), w_glu.astype(BF16))
    return out, fin[:, 0].reshape(batch, g, p), fin[:, 1].reshape(batch, g, p)


_RW_OPS = ("r", "k", "v", "d", "kk", "ka")


def _head_sum_matrix():
    hid = np.arange(RW_WIDTH) // RW_HEAD_DIM
    return jnp.asarray((hid[:, None] == hid[None, :]).astype(np.float32))


def _rwkv_pre_kernel(*refs, with_vres):
    if with_vres:
        (pc_ref, prev_ref, vf_ref, mu_ref, w0_ref, w2_ref, a0_ref, a2_ref, g2_ref, kk_ref, ka_ref,
         rk_ref, j_ref, v0_ref, v1_ref, v2_ref, ops_ref, gb_ref) = refs
    else:
        (pc_ref, prev_ref, mu_ref, w0_ref, w2_ref, a0_ref, a2_ref, g2_ref, kk_ref, ka_ref,
         rk_ref, j_ref, ops_ref, gb_ref) = refs
    pc = pc_ref[...]
    xm = pc + (prev_ref[...] - pc) * mu_ref[...]
    w_ = RW_WIDTH
    r, k, v = xm[:, 0:w_], xm[:, w_:2 * w_], xm[:, 2 * w_:3 * w_]
    o = 3 * w_
    lora = xm[:, o:o + RW_DECAY_LORA + RW_AAA_LORA]
    gl = xm[:, o + RW_DECAY_LORA + RW_AAA_LORA:]
    mm = lambda a, b_ref: jnp.dot(a.astype(BF16), b_ref[...].astype(BF16), preferred_element_type=F32)
    z = -(w0_ref[...] + mm(jnp.tanh(lora), w2_ref))
    softplus = jnp.maximum(z, 0.0) + jnp.log(1.0 + jnp.exp(-jnp.abs(z)))
    w = -softplus - 0.5
    decay = jnp.exp(-jnp.exp(w))
    a = jax.nn.sigmoid(a0_ref[...] + mm(lora, a2_ref))
    g = mm(jax.nn.sigmoid(gl), g2_ref)
    if with_vres:
        v = v + (vf_ref[...] - v) * jax.nn.sigmoid(v0_ref[...] + mm(mm(v, v1_ref), v2_ref))
    jm = j_ref[...]
    kk = k * kk_ref[...]
    kk = kk * lax.rsqrt(jnp.maximum(_dot_exact(kk * kk, jm), 1e-24))
    k = k * (1.0 + (a - 1.0) * ka_ref[...])
    bonus = _dot_exact(r * k * rk_ref[...], jm) * v
    for idx, val in enumerate((r, k, v, decay, kk, kk * a)):
        ops_ref[:, idx * w_:(idx + 1) * w_] = val
    gb_ref[:, 0:w_] = g
    gb_ref[:, w_:2 * w_] = bonus


def _rwkv_scan_kernel(ops_ref, s0_ref, mask_ref, j_ref, y_ref, fin_ref, st_ref, zh_ref, zl_ref, e_ref,
                      *, tc, sub):
    @pl.when(pl.program_id(1) == 0)
    def _():
        st_ref[...] = s0_ref[0]

    n, w_ = RW_HEAD_DIM, RW_WIDTH
    col = {name: idx * w_ for idx, name in enumerate(_RW_OPS)}
    expand = ("d", "kk", "ka", "k", "r")
    mask = mask_ref[...] > 0.5
    jm = j_ref[...]

    def sub_chunk(sc, state):
        t0 = pl.multiple_of(sc * sub, sub)
        for oi, name in enumerate(expand):
            for t in range(sub):
                row = ops_ref[0, pl.ds(t0 + t, 1), col[name]:col[name] + w_]
                hi = row.astype(BF16).astype(F32)
                lo = row - hi
                zh_ref[t * n:(t + 1) * n, :] = jnp.where(mask, jnp.broadcast_to(hi, (n, w_)), 0.0).astype(BF16)
                zl_ref[t * n:(t + 1) * n, :] = jnp.where(mask, jnp.broadcast_to(lo, (n, w_)), 0.0).astype(BF16)
            e_ref[oi] = (jnp.dot(zh_ref[...], jm, preferred_element_type=F32)
                         + jnp.dot(zl_ref[...], jm, preferred_element_type=F32))
        for t in range(sub):
            rows = slice(t * n, (t + 1) * n)
            vrow = ops_ref[0, pl.ds(t0 + t, 1), col["v"]:col["v"] + w_]
            sa = -jnp.sum(state * e_ref[1, rows, :], axis=0, keepdims=True)
            state = e_ref[0, rows, :] * state + e_ref[2, rows, :] * sa + e_ref[3, rows, :] * vrow
            y_ref[0, pl.ds(t0 + t, 1), :] = jnp.sum(state * e_ref[4, rows, :], axis=0, keepdims=True)
        return state

    state = lax.fori_loop(0, tc // sub, sub_chunk, st_ref[...])
    st_ref[...] = state
    fin_ref[0] = state


def _rwkv_post_kernel(y_ref, gb_ref, lw_ref, lb_ref, j_ref, o_ref):
    y = y_ref[...]
    jm = j_ref[...]
    inv = 1.0 / RW_HEAD_DIM
    mean = _dot_exact(y, jm) * inv
    yc = y - mean
    var = _dot_exact(yc * yc, jm) * inv
    yn = yc * lax.rsqrt(var + RW_GN_EPS) * lw_ref[...] + lb_ref[...]
    gb = gb_ref[...]
    o_ref[...] = (yn + gb[:, RW_WIDTH:]) * gb[:, 0:RW_WIDTH]


def _rwkv(pc, shift_prev, wkv0, v_first, vres, mu, w0, w2, a0, a2, g2, k_k, k_a, r_k, ln_w, ln_b, batch):
    m = pc.shape[0]
    t = m // batch
    w_ = RW_WIDTH
    pc3 = pc.reshape(batch, t, RW_PROJ)
    prev = jnp.concatenate([shift_prev.astype(F32)[:, None], pc3[:, :-1]], axis=1).reshape(m, RW_PROJ)
    jm = _head_sum_matrix()
    row = lambda x: x.reshape(1, -1).astype(F32)
    tm = 256
    rspec = lambda width: pl.BlockSpec((tm, width), lambda i: (i, 0))
    fspec = lambda a: pl.BlockSpec(a.shape, lambda i: (0,) * a.ndim)
    w2p = jnp.pad(w2, ((0, RW_AAA_LORA), (0, 0)))
    a2p = jnp.pad(a2, ((RW_DECAY_LORA, 0), (0, 0)))
    params = [row(mu), row(w0), w2p, row(a0), a2p, g2, row(k_k), row(k_a), row(r_k), jm]
    ins = [pc, prev]
    specs = [rspec(RW_PROJ), rspec(RW_PROJ)]
    if vres is not None:
        ins.append(v_first)
        specs.append(rspec(w_))
        params += [row(vres[0]), vres[1], vres[2]]
    ops, gb = pl.pallas_call(
        functools.partial(_rwkv_pre_kernel, with_vres=vres is not None),
        grid=(m // tm,),
        in_specs=specs + [fspec(a) for a in params],
        out_specs=[rspec(len(_RW_OPS) * w_), rspec(2 * w_)],
        out_shape=[jax.ShapeDtypeStruct((m, len(_RW_OPS) * w_), F32),
                   jax.ShapeDtypeStruct((m, 2 * w_), F32)],
        compiler_params=_cparams(("parallel",)),
        name="rwkv_pre",
    )(*ins, *params)
    if vres is None:
        v_first = ops[:, 2 * w_:3 * w_]

    n = RW_HEAD_DIM
    sub = 8
    tc = min(t, 64)
    jidx = np.arange(w_) % n
    mask = jnp.asarray((np.arange(n)[:, None] == jidx[None, :]).astype(np.float32))
    p0 = wkv0.astype(F32).transpose(0, 3, 1, 2).reshape(batch, n, w_)
    y, fin = pl.pallas_call(
        functools.partial(_rwkv_scan_kernel, tc=tc, sub=sub),
        grid=(batch, t // tc),
        in_specs=[pl.BlockSpec((1, tc, len(_RW_OPS) * w_), lambda b, c: (b, c, 0)),
                  pl.BlockSpec((1, n, w_), lambda b, c: (b, 0, 0)),
                  pl.BlockSpec((n, w_), lambda b, c: (0, 0)),
                  pl.BlockSpec((w_, w_), lambda b, c: (0, 0))],
        out_specs=[pl.BlockSpec((1, tc, w_), lambda b, c: (b, c, 0)),
                   pl.BlockSpec((1, n, w_), lambda b, c: (b, 0, 0))],
        out_shape=[jax.ShapeDtypeStruct((batch, t, w_), F32),
                   jax.ShapeDtypeStruct((batch, n, w_), F32)],
        scratch_shapes=[pltpu.VMEM((n, w_), F32),
                        pltpu.VMEM((sub * n, w_), BF16),
                        pltpu.VMEM((sub * n, w_), BF16),
                        pltpu.VMEM((5, sub * n, w_), F32)],
        compiler_params=_cparams(("parallel", "arbitrary")),
        name="rwkv_scan",
    )(ops.reshape(batch, t, len(_RW_OPS) * w_), p0, mask, jm.astype(BF16))
    wkv = fin.reshape(batch, n, RW_HEADS, n).transpose(0, 2, 3, 1)

    out = pl.pallas_call(
        _rwkv_post_kernel,
        grid=(m // tm,),
        in_specs=[rspec(w_), rspec(2 * w_), fspec(row(ln_w)), fspec(row(ln_b)), fspec(jm)],
        out_specs=rspec(w_),
        out_shape=jax.ShapeDtypeStruct((m, w_), F32),
        compiler_params=_cparams(("parallel",)),
        name="rwkv_post",
    )(y.reshape(m, w_), gb, row(ln_w), row(ln_b), jm)
    return out, wkv, v_first


def _relayout_w_in(w_in):
    cols = []
    for name, width in _PIECES:
        src, _, _ = _LAYOUT[name]
        piece = w_in[:, src:src + width]
        pad = -(-width // LANES) * LANES - width
        cols.append(jnp.pad(piece, ((0, 0), (0, pad))) if pad else piece)
    return jnp.concatenate(cols, axis=1).astype(BF16)


def _proj_scale_row():
    s = np.ones((1, PROJ_COLS), np.float32)
    _, d0, w0 = _LAYOUT["qi"]
    s[0, d0:d0 + w0] = IDX_DIM ** -0.5
    _, d1, w1 = _LAYOUT["wi"]
    s[0, d1:d1 + w1] = IDX_HEADS ** -0.5
    return jnp.asarray(s)


def _trunk(x, ple, attend, s5_re0, s5_im0, wkv0, shift0, W):
    batch, t, _ = x.shape
    m = batch * t
    depth = W["w_in"].shape[0]
    h = x.reshape(m, D_MODEL)
    v_first = None
    ks, vs, kis, sres, sims, wkvs, shifts = [], [], [], [], [], [], []
    scale_row = _proj_scale_row()
    ones_row = jnp.ones((1, 3 * D_MODEL), F32)
    y = None

    def piece(proj, name):
        _, dst, width = _LAYOUT[name]
        return proj[:, dst:dst + width]

    for i in range(depth):
        proj = _norm_mm(h, W["norm_mix"][i], _relayout_w_in(W["w_in"][i]), scale_row)
        gates = _norm_mm(h, W["norm_mix"][i], W["w_gate"][i].astype(BF16), ones_row, act="sigmoid")
        q, k, v, qi, wi, ki, u, pc = (piece(proj, n) for n, _ in _PIECES)
        o_a = attend(i, q, k, v, qi, wi, ki)
        o_b, s_re, s_im = _s5(u, s5_re0[i], s5_im0[i], W["ssm_lambda_re"][i], W["ssm_lambda_im"][i],
                              W["ssm_log_step"][i], W["ssm_b_re"][i], W["ssm_b_im"][i],
                              W["ssm_c_re"][i], W["ssm_c_im"][i], W["ssm_d"][i], W["ssm_w_glu"][i], batch)
        vres = None if i == 0 else (W["rw_v0"][i - 1], W["rw_v1"][i - 1], W["rw_v2"][i - 1])
        o_c, wkv, v_first = _rwkv(pc, shift0[i], wkv0[i], v_first, vres, W["rw_mu"][i], W["rw_w0"][i],
                                  W["rw_w2"][i], W["rw_a0"][i], W["rw_a2"][i], W["rw_g2"][i],
                                  W["rw_k_k"][i], W["rw_k_a"][i], W["rw_r_k"][i], W["rw_ln_w"][i],
                                  W["rw_ln_b"][i], batch)
        h = _merge(o_a, o_b, o_c, gates, h, W["w_up_a"][i].astype(BF16), W["w_up_b"][i].astype(BF16),
                   W["w_up_c"][i].astype(BF16), W["w_out"][i].astype(BF16))
        h = _ffn(h, W["norm_ffn"][i], W["w_ff1"][i].astype(BF16), W["w_ff2"][i].astype(BF16))
        h, y = _ple(h, ple[i].reshape(m, -1), W["norm_ple"][i], W["norm_final"],
                    W["w_ple_gate"][i].astype(BF16), W["w_ple"][i].astype(BF16))
        ks.append(k.reshape(batch, t, A_HEADS, A_HEAD_DIM))
        vs.append(v.reshape(batch, t, A_HEADS, A_HEAD_DIM))
        kis.append(ki.reshape(batch, t, IDX_DIM))
        sres.append(s_re)
        sims.append(s_im)
        wkvs.append(wkv)
        shifts.append(pc.reshape(batch, t, RW_PROJ)[:, -1])
    return (y.reshape(batch, t, D_MODEL), jnp.stack(ks), jnp.stack(vs), jnp.stack(kis), jnp.stack(sres),
            jnp.stack(sims), jnp.stack(wkvs), jnp.stack(shifts))


def kernel(x_prompt, x_sample, cache_k, cache_v, cache_idx_k, state_ssm_re, state_ssm_im, state_wkv,
           state_shift, page_table, p_prompt, p_sample, rel_bias, norm_mix, norm_ffn, norm_ple, norm_final,
           w_in, w_up_a, w_up_b, w_up_c, w_gate, w_out, ssm_lambda_re, ssm_lambda_im, ssm_log_step,
           ssm_b_re, ssm_b_im, ssm_c_re, ssm_c_im, ssm_d, ssm_w_glu, rw_mu, rw_w0, rw_w2, rw_a0, rw_a2,
           rw_g2, rw_k_k, rw_k_a, rw_r_k, rw_ln_w, rw_ln_b, rw_v0, rw_v1, rw_v2, w_ff1, w_ff2,
           w_ple, w_ple_gate):
    W = dict(norm_mix=norm_mix, norm_ffn=norm_ffn, norm_ple=norm_ple, norm_final=norm_final, w_in=w_in,
             w_up_a=w_up_a, w_up_b=w_up_b, w_up_c=w_up_c, w_gate=w_gate, w_out=w_out,
             ssm_lambda_re=ssm_lambda_re, ssm_lambda_im=ssm_lambda_im, ssm_log_step=ssm_log_step,
             ssm_b_re=ssm_b_re, ssm_b_im=ssm_b_im, ssm_c_re=ssm_c_re, ssm_c_im=ssm_c_im, ssm_d=ssm_d,
             ssm_w_glu=ssm_w_glu, rw_mu=rw_mu, rw_w0=rw_w0, rw_w2=rw_w2, rw_a0=rw_a0, rw_a2=rw_a2,
             rw_g2=rw_g2, rw_k_k=rw_k_k, rw_k_a=rw_k_a, rw_r_k=rw_r_k, rw_ln_w=rw_ln_w, rw_ln_b=rw_ln_b,
             rw_v0=rw_v0, rw_v1=rw_v1, rw_v2=rw_v2, w_ff1=w_ff1, w_ff2=w_ff2, w_ple=w_ple,
             w_ple_gate=w_ple_gate)
    depth = w_in.shape[0]
    bp = x_prompt.shape[0]

    def attend_prompt(i, q, k, v, qi, wi, ki):
        return _dsa_prompt(q, k, v, qi, wi, ki, rel_bias)

    def attend_sample(i, q, k, v, qi, wi, ki):
        return _dsa_sample(i, q, k, v, qi, wi, ki, cache_k, cache_v, cache_idx_k, page_table, rel_bias)

    zs = jnp.zeros((depth, bp, S5_GROUPS, S5_STATE), F32)
    zw = jnp.zeros((depth, bp, RW_HEADS, RW_HEAD_DIM, RW_HEAD_DIM), F32)
    zsh = jnp.zeros((depth, bp, RW_PROJ), x_prompt.dtype)
    (y_prompt, k_prompt, v_prompt, idx_k_prompt, ssm_re_prompt, ssm_im_prompt, wkv_prompt,
     shift_prompt) = _trunk(x_prompt, p_prompt, attend_prompt, zs, zs, zw, zsh, W)
    (y_sample, k_sample, v_sample, idx_k_sample, ssm_re_sample, ssm_im_sample, wkv_sample,
     shift_sample) = _trunk(x_sample, p_sample, attend_sample, state_ssm_re, state_ssm_im, state_wkv,
                            state_shift, W)
    return (y_prompt, y_sample, k_prompt, v_prompt, idx_k_prompt, k_sample, v_sample, idx_k_sample,
            ssm_re_prompt, ssm_im_prompt, ssm_re_sample, ssm_im_sample, wkv_prompt, wkv_sample,
            shift_prompt, shift_sample)
```

```python
import functools
import math

import numpy as np
import jax
import jax.numpy as jnp
from jax import lax
from jax.experimental import pallas as pl
from jax.experimental.pallas import tpu as pltpu

F32 = jnp.float32
BF16 = jnp.bfloat16
I32 = jnp.int32

D_MODEL = 1024
A_HEADS = 8
A_HEAD_DIM = 64
A_WIDTH = A_HEADS * A_HEAD_DIM
IDX_HEADS = 8
IDX_DIM = 64
TOPK_MAX = 256
N_BUCKETS = 32
MAX_DISTANCE = 128
S5_GROUP = 16
S5_GROUPS = 16
S5_WIDTH = 256
S5_STATE = 64
RW_HEADS = 4
RW_HEAD_DIM = 64
RW_WIDTH = 256
RW_DECAY_LORA = 64
RW_AAA_LORA = 64
RW_GATE_LORA = 128
RW_PROJ = 1024
RW_GN_EPS = 64e-5
D_FF = 4 * D_MODEL
EPS = 1e-6
NEG = -1e30
INT_MIN = -(2 ** 31)
LOG2E = math.log2(math.e)
LOGIT_BOUND = 60.0

LANES = 128
SUBLANES = 8
VMEM_LIMIT = 56 * 1024 * 1024

_PIECES = (("q", A_WIDTH), ("k", A_WIDTH), ("v", A_WIDTH), ("qi", IDX_HEADS * IDX_DIM),
           ("wi", IDX_HEADS), ("ki", IDX_DIM), ("u", S5_WIDTH), ("pc", RW_PROJ))


def _piece_layout():
    src, dst, out = 0, 0, {}
    for name, width in _PIECES:
        padded = -(-width // LANES) * LANES
        out[name] = (src, dst, width)
        src += width
        dst += padded
    return out, dst


_LAYOUT, PROJ_COLS = _piece_layout()

DSA_TQ = 256
DSA_TK = 512
SAMPLE_PAGES_PER_STEP = 8


def _cparams(sem):
    return pltpu.CompilerParams(dimension_semantics=sem, vmem_limit_bytes=VMEM_LIMIT)


def _rmsnorm(x, g):
    return x * lax.rsqrt(jnp.mean(x * x, axis=-1, keepdims=True) + EPS) * g


def _dot_exact(a, b):
    return jnp.dot(a, b, precision=lax.Precision.HIGHEST, preferred_element_type=F32)


def _dot_nt(a, b):
    return lax.dot_general(a, b, (((1,), (1,)), ((), ())), preferred_element_type=F32)


def _row_tile(m):
    return 512 if m % 512 == 0 else 256


def _norm_mm_kernel(x_ref, g_ref, w_ref, s_ref, o_ref, xn_ref, *, act):
    @pl.when(pl.program_id(1) == 0)
    def _():
        xn_ref[...] = _rmsnorm(x_ref[...], g_ref[...]).astype(BF16)

    y = jnp.dot(xn_ref[...], w_ref[...], preferred_element_type=F32)
    if act == "sigmoid":
        y = jax.nn.sigmoid(y)
    o_ref[...] = y * s_ref[...]


def _norm_mm(x, g, w_bf16, scale_row, act=None, tn=512):
    m, d = x.shape
    n = w_bf16.shape[1]
    tm = _row_tile(m)
    return pl.pallas_call(
        functools.partial(_norm_mm_kernel, act=act),
        grid=(m // tm, n // tn),
        in_specs=[pl.BlockSpec((tm, d), lambda i, j: (i, 0)),
                  pl.BlockSpec((1, d), lambda i, j: (0, 0)),
                  pl.BlockSpec((d, tn), lambda i, j: (0, j)),
                  pl.BlockSpec((1, tn), lambda i, j: (0, j))],
        out_specs=pl.BlockSpec((tm, tn), lambda i, j: (i, j)),
        out_shape=jax.ShapeDtypeStruct((m, n), F32),
        scratch_shapes=[pltpu.VMEM((tm, d), BF16)],
        compiler_params=_cparams(("parallel", "arbitrary")),
        name="norm_mm",
    )(x, g.reshape(1, d), w_bf16, scale_row)


def _merge_kernel(oa_ref, ob_ref, oc_ref, g_ref, h_ref, wa_ref, wb_ref, wc_ref, wo_ref, o_ref):
    g = g_ref[...]
    ya = jnp.dot(oa_ref[...].astype(BF16), wa_ref[...], preferred_element_type=F32)
    yb = jnp.dot(ob_ref[...].astype(BF16), wb_ref[...], preferred_element_type=F32)
    yc = jnp.dot(oc_ref[...].astype(BF16), wc_ref[...], preferred_element_type=F32)
    mixed = (g[:, 0:D_MODEL] * ya + g[:, D_MODEL:2 * D_MODEL] * yb
             + g[:, 2 * D_MODEL:3 * D_MODEL] * yc)
    o_ref[...] = h_ref[...] + jnp.dot(mixed.astype(BF16), wo_ref[...], preferred_element_type=F32)


def _merge(oa, ob, oc, gates, h, wa, wb, wc, wo):
    m = h.shape[0]
    tm = 256
    row = lambda w: pl.BlockSpec((tm, w), lambda i: (i, 0))
    full = lambda a: pl.BlockSpec(a.shape, lambda i: (0, 0))
    return pl.pallas_call(
        _merge_kernel,
        grid=(m // tm,),
        in_specs=[row(A_WIDTH), row(S5_WIDTH), row(RW_WIDTH), row(3 * D_MODEL), row(D_MODEL),
                  full(wa), full(wb), full(wc), full(wo)],
        out_specs=row(D_MODEL),
        out_shape=jax.ShapeDtypeStruct((m, D_MODEL), F32),
        compiler_params=_cparams(("parallel",)),
        name="merge_out",
    )(oa, ob, oc, gates, h, wa, wb, wc, wo)


def _ffn_kernel(h_ref, g_ref, w1_ref, w2_ref, o_ref, hn_ref):
    @pl.when(pl.program_id(1) == 0)
    def _():
        h = h_ref[...]
        hn_ref[...] = _rmsnorm(h, g_ref[...]).astype(BF16)
        o_ref[...] = h

    a = jnp.dot(hn_ref[...], w1_ref[...], preferred_element_type=F32)
    a = jnp.square(jnp.maximum(a, 0.0))
    o_ref[...] += jnp.dot(a.astype(BF16), w2_ref[...], preferred_element_type=F32)


def _ffn(h, g, w1, w2, tf=512):
    m, d = h.shape
    f = w1.shape[1]
    tm = _row_tile(m)
    return pl.pallas_call(
        _ffn_kernel,
        grid=(m // tm, f // tf),
        in_specs=[pl.BlockSpec((tm, d), lambda i, j: (i, 0)),
                  pl.BlockSpec((1, d), lambda i, j: (0, 0)),
                  pl.BlockSpec((d, tf), lambda i, j: (0, j)),
                  pl.BlockSpec((tf, d), lambda i, j: (j, 0))],
        out_specs=pl.BlockSpec((tm, d), lambda i, j: (i, 0)),
        out_shape=jax.ShapeDtypeStruct((m, d), F32),
        scratch_shapes=[pltpu.VMEM((tm, d), BF16)],
        compiler_params=_cparams(("parallel", "arbitrary")),
        name="ffn",
    )(h, g.reshape(1, d), w1, w2)


def _ple_kernel(h_ref, p_ref, g_ref, gf_ref, wg_ref, wp_ref, o_ref, y_ref):
    h = h_ref[...]
    pg = jax.nn.sigmoid(jnp.dot(_rmsnorm(h, g_ref[...]).astype(BF16), wg_ref[...],
                                preferred_element_type=F32))
    e = jnp.dot(p_ref[...].astype(BF16), wp_ref[...], preferred_element_type=F32)
    out = h + e * pg
    o_ref[...] = out
    y_ref[...] = _rmsnorm(out, gf_ref[...])


def _ple(h, ple, g, g_final, wg, wp):
    m, d = h.shape
    tm = 256
    row = lambda w: pl.BlockSpec((tm, w), lambda i: (i, 0))
    full = lambda shape: pl.BlockSpec(shape, lambda i: (0, 0))
    return pl.pallas_call(
        _ple_kernel,
        grid=(m // tm,),
        in_specs=[row(d), row(ple.shape[1]), full((1, d)), full((1, d)), full(wg.shape), full(wp.shape)],
        out_specs=[row(d), row(d)],
        out_shape=[jax.ShapeDtypeStruct((m, d), F32), jax.ShapeDtypeStruct((m, d), F32)],
        compiler_params=_cparams(("parallel",)),
        name="ple",
    )(h, ple, g.reshape(1, d), g_final.reshape(1, d), wg, wp)


def _sortable_key(s):
    b = lax.bitcast_convert_type(s, I32)
    return b ^ ((b >> 31) & 0x7FFFFFFF)


def _kth_largest_key(count_ge, rows, k):
    zero = jnp.zeros((rows, 1), I32)
    thr = jnp.where(count_ge(zero) >= k, zero, jnp.full((rows, 1), INT_MIN, I32))

    def bit_step(it, thr):
        cand = thr + jnp.left_shift(jnp.int32(1), 30 - it)
        return jnp.where(count_ge(cand) >= k, cand, thr)

    return lax.fori_loop(0, 31, bit_step, thr)


def _t5_bucket(rel):
    n = jnp.maximum(rel, 0)
    max_exact = N_BUCKETS // 2
    nf = jnp.maximum(n, 1).astype(F32)
    large = max_exact + (jnp.log(nf / max_exact) / math.log(MAX_DISTANCE / max_exact)
                         * (N_BUCKETS - max_exact)).astype(I32)
    large = jnp.minimum(large, N_BUCKETS - 1)
    return jnp.where(n < max_exact, n, large)


def _bias_of_rel(rel_bias, rel):
    b = jnp.moveaxis(rel_bias[_t5_bucket(rel)], -1, 0).astype(F32)
    return jnp.where((rel >= 0)[None], b, NEG)


def _dsa_prompt_kernel(qb_ref, kb_ref, tid_ref, safe_ref, q_ref, k_ref, v_ref, qi_ref, wi_ref, ki_ref,
                       d_ref, o_ref, keys_ref, thr_ref, m_ref, acc_ref, lg_ref, *, topk, ratio):
    p = pl.program_id(0)
    i = qb_ref[p]
    j = kb_ref[p]
    tq, tk = DSA_TQ, DSA_TK
    jd = i // ratio

    @pl.when(j == 0)
    def _select():
        w = wi_ref[...]
        qpos = i * tq + lax.broadcasted_iota(I32, (tq, tk), 0)
        lane = lax.broadcasted_iota(I32, (tq, tk), 1)

        def chunk(c, carry):
            off = pl.multiple_of(c * tk, tk)
            kic = ki_ref[pl.ds(off, tk), :]
            s = jnp.zeros((tq, tk), F32)
            for h in range(IDX_HEADS):
                s = s + jnp.maximum(_dot_nt(qi_ref[h], kic), 0.0) * w[:, h:h + 1]
            s = jnp.where(off + lane <= qpos, s, NEG)
            keys_ref[c] = _sortable_key(s)
            return carry

        lax.fori_loop(0, jd + 1, chunk, 0)

        def count_ge(cand):
            def body(c, acc):
                ge = (keys_ref[c] >= cand).astype(I32)
                for t in range(tk // LANES):
                    acc = acc + ge[:, t * LANES:(t + 1) * LANES]
                return acc
            acc = lax.fori_loop(0, jd + 1, body, jnp.zeros((tq, LANES), I32))
            return jnp.sum(acc, axis=1, keepdims=True)

        thr_ref[...] = _kth_largest_key(count_ge, tq, topk)
        m_ref[...] = jnp.full(m_ref.shape, NEG, F32)
        acc_ref[...] = jnp.zeros(acc_ref.shape, F32)

    selb = jnp.where(keys_ref[j] >= thr_ref[...], 0.0, NEG)
    bounded = safe_ref[0] == 1

    @pl.when(bounded)
    def _fixed_offset():
        for h in range(A_HEADS):
            pr = jnp.exp2(_dot_nt(q_ref[h], k_ref[h]) + (selb + d_ref[0, h])).astype(BF16)
            acc_ref[h] += jnp.dot(pr, v_ref[h], preferred_element_type=F32)

    @pl.when(jnp.logical_not(bounded))
    def _running_max():
        block_max = []
        for h in range(A_HEADS):
            lg = _dot_nt(q_ref[h], k_ref[h]) + (selb + d_ref[0, h])
            lg_ref[h] = lg
            block_max.append(jnp.max(lg, axis=1, keepdims=True))
        for h in range(A_HEADS):
            m_prev = m_ref[h]
            m_new = jnp.maximum(m_prev, block_max[h])
            pr = jnp.exp2(lg_ref[h] - m_new).astype(BF16)
            acc_ref[h] = jnp.exp2(m_prev - m_new) * acc_ref[h] + jnp.dot(pr, v_ref[h],
                                                                         preferred_element_type=F32)
            m_ref[h] = m_new

    @pl.when(j == jd)
    def _finish():
        for h in range(A_HEADS):
            acc = acc_ref[h]
            o_ref[h] = acc[:, 0:A_HEAD_DIM] / acc[:, A_HEAD_DIM:A_HEAD_DIM + 1]


def _toeplitz_bias(rel_bias, base, rows, cols):
    n = rows + cols - 2
    rel = jnp.concatenate([base - jnp.arange(cols, dtype=I32),
                           base + rows - 1 - jnp.arange(rows - 1, dtype=I32)])
    z = _bias_of_rel(rel_bias, rel)
    m = jnp.tile(z, (1, rows))[:, :rows * n].reshape(z.shape[0], rows, n)
    return m[:, :, :cols]


def _prompt_bias_tiles(rel_bias):
    tq, tk = DSA_TQ, DSA_TK
    ratio = tk // tq
    tiles = [_toeplitz_bias(rel_bias, a * tq, tq, tk) for a in range(ratio)]
    tiles.append(_toeplitz_bias(rel_bias, tk, tq, tk))
    far = _bias_of_rel(rel_bias, jnp.full((1, 1), MAX_DISTANCE, I32))
    tiles.append(jnp.broadcast_to(far, (far.shape[0], tq, tk)))
    return jnp.stack(tiles) * LOG2E, jnp.max(jnp.abs(rel_bias)) * LOG2E


def _dsa_prompt(q, k, v, qi, wi, ki, bias):
    dtiles, bias_absmax = bias
    t = q.shape[0]
    tq, tk = DSA_TQ, DSA_TK
    ratio = tk // tq
    topk = min(TOPK_MAX, t // 4)
    n_q = t // tq
    n_kc = t // tk

    qb, kb, tid = [], [], []
    for i in range(n_q):
        jd, a = divmod(i, ratio)
        for j in range(jd + 1):
            qb.append(i)
            kb.append(j)
            tid.append(a if j == jd else (ratio if (j == jd - 1 and a == 0) else ratio + 1))
    qb, kb, tid = (jnp.asarray(np.asarray(x, np.int32)) for x in (qb, kb, tid))

    def heads(x):
        return x.reshape(t, -1, A_HEAD_DIM).transpose(1, 0, 2).astype(BF16)

    qh, kh = heads(q * (A_HEAD_DIM ** -0.5 * LOG2E)), heads(k)
    vh = heads(v)
    vh = jnp.concatenate([vh, jnp.ones((A_HEADS, t, 1), BF16),
                          jnp.zeros((A_HEADS, t, LANES - A_HEAD_DIM - 1), BF16)], axis=-1)
    row_norm = lambda x: jnp.sqrt(jnp.max(jnp.sum(jnp.square(x.astype(F32)), axis=-1), axis=-1))
    bound = jnp.max(row_norm(qh) * row_norm(kh)) + bias_absmax
    safe = (bound <= LOGIT_BOUND).astype(I32).reshape(1)

    grid_spec = pltpu.PrefetchScalarGridSpec(
        num_scalar_prefetch=4,
        grid=(int(qb.shape[0]),),
        in_specs=[
            pl.BlockSpec((A_HEADS, tq, A_HEAD_DIM), lambda p, qb, kb, tid, sf: (0, qb[p], 0)),
            pl.BlockSpec((A_HEADS, tk, A_HEAD_DIM), lambda p, qb, kb, tid, sf: (0, kb[p], 0)),
            pl.BlockSpec((A_HEADS, tk, LANES), lambda p, qb, kb, tid, sf: (0, kb[p], 0)),
            pl.BlockSpec((IDX_HEADS, tq, IDX_DIM), lambda p, qb, kb, tid, sf: (0, qb[p], 0)),
            pl.BlockSpec((tq, IDX_HEADS), lambda p, qb, kb, tid, sf: (qb[p], 0)),
            pl.BlockSpec((t, IDX_DIM), lambda p, qb, kb, tid, sf: (0, 0)),
            pl.BlockSpec((1, A_HEADS, tq, tk), lambda p, qb, kb, tid, sf: (tid[p], 0, 0, 0)),
        ],
        out_specs=pl.BlockSpec((A_HEADS, tq, A_HEAD_DIM), lambda p, qb, kb, tid, sf: (0, qb[p], 0)),
        scratch_shapes=[pltpu.VMEM((n_kc, tq, tk), I32),
                        pltpu.VMEM((tq, 1), I32),
                        pltpu.VMEM((A_HEADS, tq, 1), F32),
                        pltpu.VMEM((A_HEADS, tq, LANES), F32),
                        pltpu.VMEM((A_HEADS, tq, tk), F32)],
    )
    out = pl.pallas_call(
        functools.partial(_dsa_prompt_kernel, topk=topk, ratio=ratio),
        grid_spec=grid_spec,
        out_shape=jax.ShapeDtypeStruct((A_HEADS, t, A_HEAD_DIM), F32),
        compiler_params=_cparams(("arbitrary",)),
        name="dsa_prompt",
    )(qb, kb, tid, safe, qh, kh, vh, heads(qi), wi, ki.astype(BF16), dtiles)
    return out.transpose(1, 0, 2).reshape(t, A_WIDTH)


def _dsa_sample_select_kernel(pt_ref, qi_ref, wi_ref, *rest, n_pages, pps, n_new, topk):
    cki_refs = rest[:pps]
    kin_ref, keys_ref, thr_ref = rest[pps:]
    p = pl.program_id(1)
    rows = n_new
    page = kin_ref.shape[1]
    w = wi_ref[0][:, 0:1]

    def scores(kblk):
        r = jnp.maximum(_dot_nt(qi_ref[0], kblk.astype(BF16)), 0.0) * w
        return jnp.sum(r.reshape(IDX_HEADS, rows, r.shape[-1]), axis=0)

    s = scores(jnp.concatenate([r[0, 0] for r in cki_refs], axis=0))
    for r in range(pps):
        keys_ref[0, p * pps + r] = _sortable_key(s[:, r * page:(r + 1) * page])

    @pl.when(p == n_pages // pps - 1)
    def _new():
        s = scores(kin_ref[0])
        qrow = lax.broadcasted_iota(I32, s.shape, 0)
        col = lax.broadcasted_iota(I32, s.shape, 1)
        keys_ref[0, n_pages] = _sortable_key(jnp.where(col <= qrow, s, NEG))

        def count_ge(cand):
            def body(c, acc):
                return acc + (keys_ref[0, c] >= cand).astype(I32)
            acc = lax.fori_loop(0, n_pages + 1, body, jnp.zeros((rows, LANES), I32))
            return jnp.sum(acc, axis=1, keepdims=True)

        thr = _kth_largest_key(count_ge, rows, topk)
        thr_ref[0] = jnp.broadcast_to(thr, (rows, LANES))


def _dsa_sample_attend_kernel(pt_ref, q_ref, keys_ref, thr_ref, *rest, n_pages, pps):
    ck_refs, cv_refs = rest[:pps], rest[pps:2 * pps]
    (kn_ref, vn_ref, dfar_ref, dlast_ref, dnew_ref, rep_ref, o_ref, m_ref, l_ref, acc_ref) = rest[2 * pps:]
    p = pl.program_id(1)
    last = p == n_pages // pps - 1
    pcols = rep_ref.shape[1]

    @pl.when(p == 0)
    def _init():
        m_ref[...] = jnp.full(m_ref.shape, NEG, F32)
        l_ref[...] = jnp.zeros(l_ref.shape, F32)
        acc_ref[...] = jnp.zeros(acc_ref.shape, F32)

    thr = thr_ref[0]

    def logits(k2, keys_page, dtile):
        sel = (keys_page >= thr).astype(BF16)
        selx = jnp.dot(sel, rep_ref[...], preferred_element_type=F32)
        selx = jnp.concatenate([selx] * A_HEADS, axis=0)
        lg = _dot_nt(q_ref[0], k2.astype(BF16)) + dtile
        return jnp.where(selx > 0.5, lg, NEG)

    def accumulate(lgs, v2s):
        m_prev = m_ref[...]
        m_new = functools.reduce(jnp.maximum, [jnp.max(lg, axis=1, keepdims=True) for lg in lgs], m_prev)
        alpha = jnp.exp2(m_prev - m_new)
        l = alpha * l_ref[...]
        acc = alpha * acc_ref[...]
        for lg, v2 in zip(lgs, v2s):
            pr = jnp.exp2(lg - m_new)
            l = l + jnp.sum(pr, axis=1, keepdims=True)
            acc = acc + jnp.dot(pr.astype(BF16), v2.astype(BF16), preferred_element_type=F32)
        l_ref[...] = l
        acc_ref[...] = acc
        m_ref[...] = m_new

    dfar = dfar_ref[...]
    dtail = jnp.where(last, dlast_ref[...], dfar)
    lgs, v2s = [], []
    for r in range(pps):
        k2 = ck_refs[r][0, 0].reshape(pcols, A_HEAD_DIM)
        v2s.append(cv_refs[r][0, 0].reshape(pcols, A_HEAD_DIM))
        lgs.append(logits(k2, keys_ref[0, p * pps + r], dtail if r == pps - 1 else dfar))
    accumulate(lgs, v2s)

    @pl.when(last)
    def _new():
        accumulate([logits(kn_ref[0], keys_ref[0, n_pages], dnew_ref[...])], [vn_ref[0]])
        o_ref[0] = acc_ref[...] / l_ref[...]


def _sample_bias_tiles(rel_bias, n_new, page):
    hrows, pcols = A_HEADS * n_new, page * A_HEADS
    qq = jnp.arange(n_new, dtype=I32)[:, None]
    pos = jnp.arange(page, dtype=I32)[None, :]
    eye = jnp.eye(A_HEADS, dtype=bool)[:, None, None, :]

    def tile(rel, ok):
        b = jnp.where(ok[None], _bias_of_rel(rel_bias, rel), NEG)
        return jnp.where(eye, b[..., None], NEG).reshape(hrows, pcols)

    always = jnp.ones((n_new, page), bool)
    far = tile(jnp.full((n_new, page), MAX_DISTANCE, I32), always)
    last = tile(page + qq - pos, always)
    new = tile(qq - pos, pos < n_new)
    return far * LOG2E, last * LOG2E, new * LOG2E


def _dsa_sample(layer, q, k_new, v_new, qi, wi, ki_new, cache_k, cache_v, cache_idx_k,
                page_table, dtiles):
    bd, n_pages = page_table.shape
    page = cache_k.shape[2]
    n_new = q.shape[0] // bd
    past = n_pages * page
    topk = min(TOPK_MAX, (past + n_new) // 4)
    hrows = A_HEADS * n_new
    pcols = page * A_HEADS
    pps = SAMPLE_PAGES_PER_STEP
    n_steps = n_pages // pps
    pt = page_table.reshape(-1).astype(I32)

    def head_rows(x):
        return x.reshape(bd, n_new, -1, A_HEAD_DIM).transpose(0, 2, 1, 3).reshape(bd, hrows, A_HEAD_DIM)

    qi2 = head_rows(qi).astype(BF16)
    wi2 = jnp.broadcast_to(wi.reshape(bd, n_new, IDX_HEADS).transpose(0, 2, 1).reshape(bd, hrows, 1),
                           (bd, hrows, LANES))
    kin = jnp.pad(ki_new.reshape(bd, n_new, IDX_DIM), ((0, 0), (0, page - n_new), (0, 0)))

    def page_spec(block, r):
        zeros = (0,) * (len(block) - 2)
        return pl.BlockSpec(block, lambda b, p, pt: (layer, pt[b * n_pages + p * pps + r]) + zeros)

    per_batch = lambda block: pl.BlockSpec(block, lambda b, p, pt: (b,) + (0,) * (len(block) - 1))
    const = lambda block: pl.BlockSpec(block, lambda b, p, pt: (0,) * len(block))

    keys, thr = pl.pallas_call(
        functools.partial(_dsa_sample_select_kernel, n_pages=n_pages, pps=pps, n_new=n_new, topk=topk),
        grid_spec=pltpu.PrefetchScalarGridSpec(
            num_scalar_prefetch=1,
            grid=(bd, n_steps),
            in_specs=([per_batch((1, hrows, IDX_DIM)), per_batch((1, hrows, LANES))]
                      + [page_spec((1, 1, page, IDX_DIM), r) for r in range(pps)]
                      + [per_batch((1, page, IDX_DIM))]),
            out_specs=[per_batch((1, n_pages + 1, n_new, page)), per_batch((1, n_new, LANES))],
        ),
        out_shape=[jax.ShapeDtypeStruct((bd, n_pages + 1, n_new, page), I32),
                   jax.ShapeDtypeStruct((bd, n_new, LANES), I32)],
        compiler_params=_cparams(("parallel", "arbitrary")),
        name="dsa_sample_select",
    )(pt, qi2, wi2, *([cache_idx_k] * pps), kin)

    dfar, dlast, dnew = dtiles
    pos = (jnp.arange(pcols, dtype=I32) // A_HEADS)[None, :]
    rep = (jnp.arange(page, dtype=I32)[:, None] == pos).astype(BF16)

    def pos_rows(x):
        x = x.reshape(bd, n_new * A_HEADS, A_HEAD_DIM)
        return jnp.pad(x, ((0, 0), (0, pcols - n_new * A_HEADS), (0, 0)))

    cache_block = (1, 1, page, A_HEADS, A_HEAD_DIM)
    out = pl.pallas_call(
        functools.partial(_dsa_sample_attend_kernel, n_pages=n_pages, pps=pps),
        grid_spec=pltpu.PrefetchScalarGridSpec(
            num_scalar_prefetch=1,
            grid=(bd, n_steps),
            in_specs=([per_batch((1, hrows, A_HEAD_DIM)), per_batch((1, n_pages + 1, n_new, page)),
                       per_batch((1, n_new, LANES))]
                      + [page_spec(cache_block, r) for r in range(pps)]
                      + [page_spec(cache_block, r) for r in range(pps)]
                      + [per_batch((1, pcols, A_HEAD_DIM)), per_batch((1, pcols, A_HEAD_DIM)),
                         const((hrows, pcols)), const((hrows, pcols)), const((hrows, pcols)),
                         const((page, pcols))]),
            out_specs=per_batch((1, hrows, A_HEAD_DIM)),
            scratch_shapes=[pltpu.VMEM((hrows, 1), F32), pltpu.VMEM((hrows, 1), F32),
                            pltpu.VMEM((hrows, A_HEAD_DIM), F32)],
        ),
        out_shape=jax.ShapeDtypeStruct((bd, hrows, A_HEAD_DIM), F32),
        compiler_params=_cparams(("parallel", "arbitrary")),
        name="dsa_sample_attend",
    )(pt, head_rows(q * (A_HEAD_DIM ** -0.5 * LOG2E)).astype(BF16), keys, thr,
      *([cache_k] * pps), *([cache_v] * pps), pos_rows(k_new), pos_rows(v_new), dfar, dlast, dnew, rep)
    return out.reshape(bd, A_HEADS, n_new, A_HEAD_DIM).transpose(0, 2, 1, 3).reshape(bd * n_new, A_WIDTH)


def _s5_in_kernel(u_ref, bb_ref, o_ref):
    o_ref[...] = _dot_exact(u_ref[...], bb_ref[...])


def _s5_scan_kernel(bu_ref, s0_ref, ab_ref, o_ref, fin_ref, st_ref, *, tc):
    @pl.when(pl.program_id(1) == 0)
    def _():
        st_ref[...] = s0_ref[0]

    ar, ai = ab_ref[0], ab_ref[1]

    def step(t, carry):
        sr, si = carry
        nr = ar * sr - ai * si + bu_ref[0, t, 0]
        ni = ar * si + ai * sr + bu_ref[0, t, 1]
        o_ref[0, t, 0] = nr
        o_ref[0, t, 1] = ni
        return nr, ni

    sr, si = lax.fori_loop(0, tc, step, (st_ref[0], st_ref[1]), unroll=8)
    st_ref[0] = sr
    st_ref[1] = si
    fin_ref[0, 0] = sr
    fin_ref[0, 1] = si


def _s5_out_kernel(s_ref, u_ref, c_ref, d_ref, wg_ref, o_ref):
    u = u_ref[...]
    y = _dot_exact(s_ref[...], c_ref[...]) + u * d_ref[...]
    hdn = jax.nn.gelu(y)
    gate = jax.nn.sigmoid(jnp.dot(hdn.astype(BF16), wg_ref[...], preferred_element_type=F32))
    o_ref[...] = hdn * gate


def _s5(u, s0_re, s0_im, lam_re, lam_im, log_step, b_re, b_im, c_re, c_im, d, w_glu, batch):
    m = u.shape[0]
    t = m // batch
    g, p = S5_GROUPS, S5_STATE
    n_state = g * p
    lam_re = jnp.minimum(lam_re.astype(F32), -1e-4)
    lam_im = lam_im.astype(F32)
    dt = jnp.exp(log_step.astype(F32))[:, None]
    mag = jnp.exp(lam_re * dt)
    ab_re, ab_im = mag * jnp.cos(lam_im * dt), mag * jnp.sin(lam_im * dt)
    den = lam_re * lam_re + lam_im * lam_im
    nr, ni = ab_re - 1.0, ab_im
    f_re = (nr * lam_re + ni * lam_im) / den
    f_im = (ni * lam_re - nr * lam_im) / den
    bb_re = f_re[..., None] * b_re - f_im[..., None] * b_im
    bb_im = f_re[..., None] * b_im + f_im[..., None] * b_re
    eye = jnp.eye(g, dtype=F32)
    blk_in = lambda x: jnp.einsum("gph,gk->ghkp", x, eye).reshape(S5_WIDTH, n_state)
    blk_out = lambda x: jnp.einsum("ghp,gk->kpgh", x, eye).reshape(n_state, S5_WIDTH)
    bb = jnp.concatenate([blk_in(bb_re), blk_in(bb_im)], axis=1)
    cc = jnp.concatenate([blk_out(c_re.astype(F32)), -blk_out(c_im.astype(F32))], axis=0)
    ab = jnp.stack([ab_re.reshape(SUBLANES, LANES), ab_im.reshape(SUBLANES, LANES)])
    s0 = jnp.stack([s0_re.reshape(batch, SUBLANES, LANES), s0_im.reshape(batch, SUBLANES, LANES)], axis=1)

    tm = 256
    bu = pl.pallas_call(
        _s5_in_kernel,
        grid=(m // tm,),
        in_specs=[pl.BlockSpec((tm, S5_WIDTH), lambda i: (i, 0)),
                  pl.BlockSpec(bb.shape, lambda i: (0, 0))],
        out_specs=pl.BlockSpec((tm, 2 * n_state), lambda i: (i, 0)),
        out_shape=jax.ShapeDtypeStruct((m, 2 * n_state), F32),
        compiler_params=_cparams(("parallel",)),
        name="s5_in",
    )(u, bb)

    tc = min(t, 256)
    tile = (1, tc, 2, SUBLANES, LANES)
    states, fin = pl.pallas_call(
        functools.partial(_s5_scan_kernel, tc=tc),
        grid=(batch, t // tc),
        in_specs=[pl.BlockSpec(tile, lambda b, c: (b, c, 0, 0, 0)),
                  pl.BlockSpec((1, 2, SUBLANES, LANES), lambda b, c: (b, 0, 0, 0)),
                  pl.BlockSpec((2, SUBLANES, LANES), lambda b, c: (0, 0, 0))],
        out_specs=[pl.BlockSpec(tile, lambda b, c: (b, c, 0, 0, 0)),
                   pl.BlockSpec((1, 2, SUBLANES, LANES), lambda b, c: (b, 0, 0, 0))],
        out_shape=[jax.ShapeDtypeStruct((batch, t, 2, SUBLANES, LANES), F32),
                   jax.ShapeDtypeStruct((batch, 2, SUBLANES, LANES), F32)],
        scratch_shapes=[pltpu.VMEM((2, SUBLANES, LANES), F32)],
        compiler_params=_cparams(("parallel", "arbitrary")),
        name="s5_scan",
    )(bu.reshape(batch, t, 2, SUBLANES, LANES), s0, ab)

    out = pl.pallas_call(
        _s5_out_kernel,
        grid=(m // tm,),
        in_specs=[pl.BlockSpec((tm, 2 * n_state), lambda i: (i, 0)),
                  pl.BlockSpec((tm, S5_WIDTH), lambda i: (i, 0)),
                  pl.BlockSpec(cc.shape, lambda i: (0, 0)),
                  pl.BlockSpec((1, S5_WIDTH), lambda i: (0, 0)),
                  pl.BlockSpec((S5_WIDTH, S5_WIDTH), lambda i: (0, 0))],
        out_specs=pl.BlockSpec((tm, S5_WIDTH), lambda i: (i, 0)),
        out_shape=jax.ShapeDtypeStruct((m, S5_WIDTH), F32),
        compiler_params=_cparams(("parallel",)),
        name="s5_out",
    )(states.reshape(m, 2 * n_state), u, cc, d.reshape(1, S5_WIDTH).astype(F32), w_glu.astype(BF16))
    return out, fin[:, 0].reshape(batch, g, p), fin[:, 1].reshape(batch, g, p)


_RW_OPS = ("r", "k", "v", "d", "kk", "ka")


def _head_sum_matrix():
    hid = np.arange(RW_WIDTH) // RW_HEAD_DIM
    return jnp.asarray((hid[:, None] == hid[None, :]).astype(np.float32))


def _rwkv_pre_kernel(*refs, with_vres):
    if with_vres:
        (pc_ref, prev_ref, vf_ref, mu_ref, w0_ref, w2_ref, a0_ref, a2_ref, g2_ref, kk_ref, ka_ref,
         rk_ref, j_ref, v0_ref, v1_ref, v2_ref, ops_ref, gb_ref) = refs
    else:
        (pc_ref, prev_ref, mu_ref, w0_ref, w2_ref, a0_ref, a2_ref, g2_ref, kk_ref, ka_ref,
         rk_ref, j_ref, ops_ref, gb_ref) = refs
    pc = pc_ref[...]
    xm = pc + (prev_ref[...] - pc) * mu_ref[...]
    w_ = RW_WIDTH
    r, k, v = xm[:, 0:w_], xm[:, w_:2 * w_], xm[:, 2 * w_:3 * w_]
    o = 3 * w_
    lora = xm[:, o:o + RW_DECAY_LORA + RW_AAA_LORA]
    gl = xm[:, o + RW_DECAY_LORA + RW_AAA_LORA:]
    mm = lambda a, b_ref: jnp.dot(a.astype(BF16), b_ref[...].astype(BF16), preferred_element_type=F32)
    z = -(w0_ref[...] + mm(jnp.tanh(lora), w2_ref))
    softplus = jnp.maximum(z, 0.0) + jnp.log(1.0 + jnp.exp(-jnp.abs(z)))
    w = -softplus - 0.5
    decay = jnp.exp(-jnp.exp(w))
    a = jax.nn.sigmoid(a0_ref[...] + mm(lora, a2_ref))
    g = mm(jax.nn.sigmoid(gl), g2_ref)
    if with_vres:
        v = v + (vf_ref[...] - v) * jax.nn.sigmoid(v0_ref[...] + mm(mm(v, v1_ref), v2_ref))
    jm = j_ref[...]
    kk = k * kk_ref[...]
    kk = kk * lax.rsqrt(jnp.maximum(_dot_exact(kk * kk, jm), 1e-24))
    k = k * (1.0 + (a - 1.0) * ka_ref[...])
    bonus = _dot_exact(r * k * rk_ref[...], jm) * v
    for idx, val in enumerate((r, k, v, decay, kk, kk * a)):
        ops_ref[:, idx * w_:(idx + 1) * w_] = val
    gb_ref[:, 0:w_] = g
    gb_ref[:, w_:2 * w_] = bonus


def _rwkv_scan_kernel(ops_ref, s0_ref, mask_ref, j_ref, y_ref, fin_ref, st_ref, zh_ref, zl_ref, e_ref,
                      *, tc, sub):
    @pl.when(pl.program_id(1) == 0)
    def _():
        st_ref[...] = s0_ref[0]

    n, w_ = RW_HEAD_DIM, RW_WIDTH
    col = {name: idx * w_ for idx, name in enumerate(_RW_OPS)}
    expand = ("d", "kk", "ka", "k", "r")
    mask = mask_ref[...] > 0.5
    jm = j_ref[...]

    def sub_chunk(sc, state):
        t0 = pl.multiple_of(sc * sub, sub)
        for oi, name in enumerate(expand):
            for t in range(sub):
                row = ops_ref[0, pl.ds(t0 + t, 1), col[name]:col[name] + w_]
                hi = row.astype(BF16).astype(F32)
                lo = row - hi
                zh_ref[t * n:(t + 1) * n, :] = jnp.where(mask, jnp.broadcast_to(hi, (n, w_)), 0.0).astype(BF16)
                zl_ref[t * n:(t + 1) * n, :] = jnp.where(mask, jnp.broadcast_to(lo, (n, w_)), 0.0).astype(BF16)
            e_ref[oi] = (jnp.dot(zh_ref[...], jm, preferred_element_type=F32)
                         + jnp.dot(zl_ref[...], jm, preferred_element_type=F32))
        for t in range(sub):
            rows = slice(t * n, (t + 1) * n)
            vrow = ops_ref[0, pl.ds(t0 + t, 1), col["v"]:col["v"] + w_]
            sa = -jnp.sum(state * e_ref[1, rows, :], axis=0, keepdims=True)
            state = e_ref[0, rows, :] * state + e_ref[2, rows, :] * sa + e_ref[3, rows, :] * vrow
            y_ref[0, pl.ds(t0 + t, 1), :] = jnp.sum(state * e_ref[4, rows, :], axis=0, keepdims=True)
        return state

    state = lax.fori_loop(0, tc // sub, sub_chunk, st_ref[...])
    st_ref[...] = state
    fin_ref[0] = state


def _rwkv_post_kernel(y_ref, gb_ref, lw_ref, lb_ref, j_ref, o_ref):
    y = y_ref[...]
    jm = j_ref[...]
    inv = 1.0 / RW_HEAD_DIM
    mean = _dot_exact(y, jm) * inv
    yc = y - mean
    var = _dot_exact(yc * yc, jm) * inv
    yn = yc * lax.rsqrt(var + RW_GN_EPS) * lw_ref[...] + lb_ref[...]
    gb = gb_ref[...]
    o_ref[...] = (yn + gb[:, RW_WIDTH:]) * gb[:, 0:RW_WIDTH]


def _rwkv(pc, shift_prev, wkv0, v_first, vres, mu, w0, w2, a0, a2, g2, k_k, k_a, r_k, ln_w, ln_b, batch):
    m = pc.shape[0]
    t = m // batch
    w_ = RW_WIDTH
    pc3 = pc.reshape(batch, t, RW_PROJ)
    prev = jnp.concatenate([shift_prev.astype(F32)[:, None], pc3[:, :-1]], axis=1).reshape(m, RW_PROJ)
    jm = _head_sum_matrix()
    row = lambda x: x.reshape(1, -1).astype(F32)
    tm = 256
    rspec = lambda width: pl.BlockSpec((tm, width), lambda i: (i, 0))
    fspec = lambda a: pl.BlockSpec(a.shape, lambda i: (0,) * a.ndim)
    w2p = jnp.pad(w2, ((0, RW_AAA_LORA), (0, 0)))
    a2p = jnp.pad(a2, ((RW_DECAY_LORA, 0), (0, 0)))
    params = [row(mu), row(w0), w2p, row(a0), a2p, g2, row(k_k), row(k_a), row(r_k), jm]
    ins = [pc, prev]
    specs = [rspec(RW_PROJ), rspec(RW_PROJ)]
    if vres is not None:
        ins.append(v_first)
        specs.append(rspec(w_))
        params += [row(vres[0]), vres[1], vres[2]]
    ops, gb = pl.pallas_call(
        functools.partial(_rwkv_pre_kernel, with_vres=vres is not None),
        grid=(m // tm,),
        in_specs=specs + [fspec(a) for a in params],
        out_specs=[rspec(len(_RW_OPS) * w_), rspec(2 * w_)],
        out_shape=[jax.ShapeDtypeStruct((m, len(_RW_OPS) * w_), F32),
                   jax.ShapeDtypeStruct((m, 2 * w_), F32)],
        compiler_params=_cparams(("parallel",)),
        name="rwkv_pre",
    )(*ins, *params)
    if vres is None:
        v_first = ops[:, 2 * w_:3 * w_]

    n = RW_HEAD_DIM
    sub = 8
    tc = min(t, 64)
    jidx = np.arange(w_) % n
    mask = jnp.asarray((np.arange(n)[:, None] == jidx[None, :]).astype(np.float32))
    p0 = wkv0.astype(F32).transpose(0, 3, 1, 2).reshape(batch, n, w_)
    y, fin = pl.pallas_call(
        functools.partial(_rwkv_scan_kernel, tc=tc, sub=sub),
        grid=(batch, t // tc),
        in_specs=[pl.BlockSpec((1, tc, len(_RW_OPS) * w_), lambda b, c: (b, c, 0)),
                  pl.BlockSpec((1, n, w_), lambda b, c: (b, 0, 0)),
                  pl.BlockSpec((n, w_), lambda b, c: (0, 0)),
                  pl.BlockSpec((w_, w_), lambda b, c: (0, 0))],
        out_specs=[pl.BlockSpec((1, tc, w_), lambda b, c: (b, c, 0)),
                   pl.BlockSpec((1, n, w_), lambda b, c: (b, 0, 0))],
        out_shape=[jax.ShapeDtypeStruct((batch, t, w_), F32),
                   jax.ShapeDtypeStruct((batch, n, w_), F32)],
        scratch_shapes=[pltpu.VMEM((n, w_), F32),
                        pltpu.VMEM((sub * n, w_), BF16),
                        pltpu.VMEM((sub * n, w_), BF16),
                        pltpu.VMEM((5, sub * n, w_), F32)],
        compiler_params=_cparams(("parallel", "arbitrary")),
        name="rwkv_scan",
    )(ops.reshape(batch, t, len(_RW_OPS) * w_), p0, mask, jm.astype(BF16))
    wkv = fin.reshape(batch, n, RW_HEADS, n).transpose(0, 2, 3, 1)

    out = pl.pallas_call(
        _rwkv_post_kernel,
        grid=(m // tm,),
        in_specs=[rspec(w_), rspec(2 * w_), fspec(row(ln_w)), fspec(row(ln_b)), fspec(jm)],
        out_specs=rspec(w_),
        out_shape=jax.ShapeDtypeStruct((m, w_), F32),
        compiler_params=_cparams(("parallel",)),
        name="rwkv_post",
    )(y.reshape(m, w_), gb, row(ln_w), row(ln_b), jm)
    return out, wkv, v_first


def _relayout_w_in(w_in):
    cols = []
    for name, width in _PIECES:
        src, _, _ = _LAYOUT[name]
        piece = w_in[:, src:src + width]
        pad = -(-width // LANES) * LANES - width
        cols.append(jnp.pad(piece, ((0, 0), (0, pad))) if pad else piece)
    return jnp.concatenate(cols, axis=1).astype(BF16)


def _proj_scale_row():
    s = np.ones((1, PROJ_COLS), np.float32)
    _, d0, w0 = _LAYOUT["qi"]
    s[0, d0:d0 + w0] = IDX_DIM ** -0.5
    _, d1, w1 = _LAYOUT["wi"]
    s[0, d1:d1 + w1] = IDX_HEADS ** -0.5
    return jnp.asarray(s)


def _trunk(x, ple, attend, s5_re0, s5_im0, wkv0, shift0, W):
    batch, t, _ = x.shape
    m = batch * t
    depth = W["w_in"].shape[0]
    h = x.reshape(m, D_MODEL)
    v_first = None
    ks, vs, kis, sres, sims, wkvs, shifts = [], [], [], [], [], [], []
    scale_row = _proj_scale_row()
    ones_row = jnp.ones((1, 3 * D_MODEL), F32)
    y = None

    def piece(proj, name):
        _, dst, width = _LAYOUT[name]
        return proj[:, dst:dst + width]

    for i in range(depth):
        proj = _norm_mm(h, W["norm_mix"][i], _relayout_w_in(W["w_in"][i]), scale_row)
        gates = _norm_mm(h, W["norm_mix"][i], W["w_gate"][i].astype(BF16), ones_row, act="sigmoid")
        q, k, v, qi, wi, ki, u, pc = (piece(proj, n) for n, _ in _PIECES)
        o_a = attend(i, q, k, v, qi, wi, ki)
        o_b, s_re, s_im = _s5(u, s5_re0[i], s5_im0[i], W["ssm_lambda_re"][i], W["ssm_lambda_im"][i],
                              W["ssm_log_step"][i], W["ssm_b_re"][i], W["ssm_b_im"][i],
                              W["ssm_c_re"][i], W["ssm_c_im"][i], W["ssm_d"][i], W["ssm_w_glu"][i], batch)
        vres = None if i == 0 else (W["rw_v0"][i - 1], W["rw_v1"][i - 1], W["rw_v2"][i - 1])
        o_c, wkv, v_first = _rwkv(pc, shift0[i], wkv0[i], v_first, vres, W["rw_mu"][i], W["rw_w0"][i],
                                  W["rw_w2"][i], W["rw_a0"][i], W["rw_a2"][i], W["rw_g2"][i],
                                  W["rw_k_k"][i], W["rw_k_a"][i], W["rw_r_k"][i], W["rw_ln_w"][i],
                                  W["rw_ln_b"][i], batch)
        h = _merge(o_a, o_b, o_c, gates, h, W["w_up_a"][i].astype(BF16), W["w_up_b"][i].astype(BF16),
                   W["w_up_c"][i].astype(BF16), W["w_out"][i].astype(BF16))
        h = _ffn(h, W["norm_ffn"][i], W["w_ff1"][i].astype(BF16), W["w_ff2"][i].astype(BF16))
        h, y = _ple(h, ple[i].reshape(m, -1), W["norm_ple"][i], W["norm_final"],
                    W["w_ple_gate"][i].astype(BF16), W["w_ple"][i].astype(BF16))
        ks.append(k.reshape(batch, t, A_HEADS, A_HEAD_DIM))
        vs.append(v.reshape(batch, t, A_HEADS, A_HEAD_DIM))
        kis.append(ki.reshape(batch, t, IDX_DIM))
        sres.append(s_re)
        sims.append(s_im)
        wkvs.append(wkv)
        shifts.append(pc.reshape(batch, t, RW_PROJ)[:, -1])
    return (y.reshape(batch, t, D_MODEL), jnp.stack(ks), jnp.stack(vs), jnp.stack(kis), jnp.stack(sres),
            jnp.stack(sims), jnp.stack(wkvs), jnp.stack(shifts))


def kernel(x_prompt, x_sample, cache_k, cache_v, cache_idx_k, state_ssm_re, state_ssm_im, state_wkv,
           state_shift, page_table, p_prompt, p_sample, rel_bias, norm_mix, norm_ffn, norm_ple, norm_final,
           w_in, w_up_a, w_up_b, w_up_c, w_gate, w_out, ssm_lambda_re, ssm_lambda_im, ssm_log_step,
           ssm_b_re, ssm_b_im, ssm_c_re, ssm_c_im, ssm_d, ssm_w_glu, rw_mu, rw_w0, rw_w2, rw_a0, rw_a2,
           rw_g2, rw_k_k, rw_k_a, rw_r_k, rw_ln_w, rw_ln_b, rw_v0, rw_v1, rw_v2, w_ff1, w_ff2,
           w_ple, w_ple_gate):
    W = dict(norm_mix=norm_mix, norm_ffn=norm_ffn, norm_ple=norm_ple, norm_final=norm_final, w_in=w_in,
             w_up_a=w_up_a, w_up_b=w_up_b, w_up_c=w_up_c, w_gate=w_gate, w_out=w_out,
             ssm_lambda_re=ssm_lambda_re, ssm_lambda_im=ssm_lambda_im, ssm_log_step=ssm_log_step,
             ssm_b_re=ssm_b_re, ssm_b_im=ssm_b_im, ssm_c_re=ssm_c_re, ssm_c_im=ssm_c_im, ssm_d=ssm_d,
             ssm_w_glu=ssm_w_glu, rw_mu=rw_mu, rw_w0=rw_w0, rw_w2=rw_w2, rw_a0=rw_a0, rw_a2=rw_a2,
             rw_g2=rw_g2, rw_k_k=rw_k_k, rw_k_a=rw_k_a, rw_r_k=rw_r_k, rw_ln_w=rw_ln_w, rw_ln_b=rw_ln_b,
             rw_v0=rw_v0, rw_v1=rw_v1, rw_v2=rw_v2, w_ff1=w_ff1, w_ff2=w_ff2, w_ple=w_ple,
             w_ple_gate=w_ple_gate)
    depth = w_in.shape[0]
    bp = x_prompt.shape[0]

    prompt_tiles = _prompt_bias_tiles(rel_bias)
    sample_tiles = _sample_bias_tiles(rel_bias, x_sample.shape[1], cache_k.shape[2])

    def attend_prompt(i, q, k, v, qi, wi, ki):
        return _dsa_prompt(q, k, v, qi, wi, ki, prompt_tiles)

    def attend_sample(i, q, k, v, qi, wi, ki):
        return _dsa_sample(i, q, k, v, qi, wi, ki, cache_k, cache_v, cache_idx_k, page_table, sample_tiles)

    zs = jnp.zeros((depth, bp, S5_GROUPS, S5_STATE), F32)
    zw = jnp.zeros((depth, bp, RW_HEADS, RW_HEAD_DIM, RW_HEAD_DIM), F32)
    zsh = jnp.zeros((depth, bp, RW_PROJ), x_prompt.dtype)
    (y_prompt, k_prompt, v_prompt, idx_k_prompt, ssm_re_prompt, ssm_im_prompt, wkv_prompt,
     shift_prompt) = _trunk(x_prompt, p_prompt, attend_prompt, zs, zs, zw, zsh, W)
    (y_sample, k_sample, v_sample, idx_k_sample, ssm_re_sample, ssm_im_sample, wkv_sample,
     shift_sample) = _trunk(x_sample, p_sample, attend_sample, state_ssm_re, state_ssm_im, state_wkv,
                            state_shift, W)
    return (y_prompt, y_sample, k_prompt, v_prompt, idx_k_prompt, k_sample, v_sample, idx_k_sample,
            ssm_re_prompt, ssm_im_prompt, ssm_re_sample, ssm_im_sample, wkv_prompt, wkv_sample,
            shift_prompt, shift_sample)
```

```python
import functools
import math

import numpy as np
import jax
import jax.numpy as jnp
from jax import lax
from jax.experimental import pallas as pl
from jax.experimental.pallas import tpu as pltpu

F32 = jnp.float32
BF16 = jnp.bfloat16
I32 = jnp.int32

D_MODEL = 1024
A_HEADS = 8
A_HEAD_DIM = 64
A_WIDTH = A_HEADS * A_HEAD_DIM
IDX_HEADS = 8
IDX_DIM = 64
TOPK_MAX = 256
N_BUCKETS = 32
MAX_DISTANCE = 128
S5_GROUP = 16
S5_GROUPS = 16
S5_WIDTH = 256
S5_STATE = 64
RW_HEADS = 4
RW_HEAD_DIM = 64
RW_WIDTH = 256
RW_DECAY_LORA = 64
RW_AAA_LORA = 64
RW_GATE_LORA = 128
RW_PROJ = 1024
RW_GN_EPS = 64e-5
D_FF = 4 * D_MODEL
EPS = 1e-6
NEG = -1e30
INT_MIN = -(2 ** 31)
LOG2E = math.log2(math.e)
LOGIT_BOUND = 60.0

LANES = 128
SUBLANES = 8
VMEM_LIMIT = 56 * 1024 * 1024

_PIECES = (("q", A_WIDTH), ("k", A_WIDTH), ("v", A_WIDTH), ("qi", IDX_HEADS * IDX_DIM),
           ("wi", IDX_HEADS), ("ki", IDX_DIM), ("u", S5_WIDTH), ("pc", RW_PROJ))


def _piece_layout():
    src, dst, out = 0, 0, {}
    for name, width in _PIECES:
        padded = -(-width // LANES) * LANES
        out[name] = (src, dst, width)
        src += width
        dst += padded
    return out, dst


_LAYOUT, PROJ_COLS = _piece_layout()

DSA_TQ = 256
DSA_TK = 512
SAMPLE_PAGES_PER_STEP = 8
COUNT_ROWS = 128
SEARCH_FIXED_BITS = 18


def _cparams(sem):
    return pltpu.CompilerParams(dimension_semantics=sem, vmem_limit_bytes=VMEM_LIMIT)


def _rmsnorm(x, g):
    return x * lax.rsqrt(jnp.mean(x * x, axis=-1, keepdims=True) + EPS) * g


def _dot_exact(a, b):
    return jnp.dot(a, b, precision=lax.Precision.HIGHEST, preferred_element_type=F32)


def _dot_nt(a, b):
    return lax.dot_general(a, b, (((1,), (1,)), ((), ())), preferred_element_type=F32)


def _row_tile(m):
    return 512 if m % 512 == 0 else 256


def _norm_mm_kernel(x_ref, g_ref, w_ref, s_ref, o_ref, xn_ref, *, act):
    @pl.when(pl.program_id(1) == 0)
    def _():
        xn_ref[...] = _rmsnorm(x_ref[...], g_ref[...]).astype(BF16)

    y = jnp.dot(xn_ref[...], w_ref[...], preferred_element_type=F32)
    if act == "sigmoid":
        y = jax.nn.sigmoid(y)
    o_ref[...] = y * s_ref[...]


def _norm_mm(x, g, w_bf16, scale_row, act=None, tn=512):
    m, d = x.shape
    n = w_bf16.shape[1]
    tm = _row_tile(m)
    return pl.pallas_call(
        functools.partial(_norm_mm_kernel, act=act),
        grid=(m // tm, n // tn),
        in_specs=[pl.BlockSpec((tm, d), lambda i, j: (i, 0)),
                  pl.BlockSpec((1, d), lambda i, j: (0, 0)),
                  pl.BlockSpec((d, tn), lambda i, j: (0, j)),
                  pl.BlockSpec((1, tn), lambda i, j: (0, j))],
        out_specs=pl.BlockSpec((tm, tn), lambda i, j: (i, j)),
        out_shape=jax.ShapeDtypeStruct((m, n), F32),
        scratch_shapes=[pltpu.VMEM((tm, d), BF16)],
        compiler_params=_cparams(("parallel", "arbitrary")),
        name="norm_mm",
    )(x, g.reshape(1, d), w_bf16, scale_row)


def _merge_kernel(oa_ref, ob_ref, oc_ref, g_ref, h_ref, wa_ref, wb_ref, wc_ref, wo_ref, o_ref):
    g = g_ref[...]
    ya = jnp.dot(oa_ref[...].astype(BF16), wa_ref[...], preferred_element_type=F32)
    yb = jnp.dot(ob_ref[...].astype(BF16), wb_ref[...], preferred_element_type=F32)
    yc = jnp.dot(oc_ref[...].astype(BF16), wc_ref[...], preferred_element_type=F32)
    mixed = (g[:, 0:D_MODEL] * ya + g[:, D_MODEL:2 * D_MODEL] * yb
             + g[:, 2 * D_MODEL:3 * D_MODEL] * yc)
    o_ref[...] = h_ref[...] + jnp.dot(mixed.astype(BF16), wo_ref[...], preferred_element_type=F32)


def _merge(oa, ob, oc, gates, h, wa, wb, wc, wo):
    m = h.shape[0]
    tm = 256
    row = lambda w: pl.BlockSpec((tm, w), lambda i: (i, 0))
    full = lambda a: pl.BlockSpec(a.shape, lambda i: (0, 0))
    return pl.pallas_call(
        _merge_kernel,
        grid=(m // tm,),
        in_specs=[row(A_WIDTH), row(S5_WIDTH), row(RW_WIDTH), row(3 * D_MODEL), row(D_MODEL),
                  full(wa), full(wb), full(wc), full(wo)],
        out_specs=row(D_MODEL),
        out_shape=jax.ShapeDtypeStruct((m, D_MODEL), F32),
        compiler_params=_cparams(("parallel",)),
        name="merge_out",
    )(oa, ob, oc, gates, h, wa, wb, wc, wo)


def _ffn_kernel(h_ref, g_ref, w1_ref, w2_ref, o_ref, hn_ref):
    @pl.when(pl.program_id(1) == 0)
    def _():
        h = h_ref[...]
        hn_ref[...] = _rmsnorm(h, g_ref[...]).astype(BF16)
        o_ref[...] = h

    a = jnp.dot(hn_ref[...], w1_ref[...], preferred_element_type=F32)
    a = jnp.square(jnp.maximum(a, 0.0))
    o_ref[...] += jnp.dot(a.astype(BF16), w2_ref[...], preferred_element_type=F32)


def _ffn(h, g, w1, w2, tf=512):
    m, d = h.shape
    f = w1.shape[1]
    tm = _row_tile(m)
    return pl.pallas_call(
        _ffn_kernel,
        grid=(m // tm, f // tf),
        in_specs=[pl.BlockSpec((tm, d), lambda i, j: (i, 0)),
                  pl.BlockSpec((1, d), lambda i, j: (0, 0)),
                  pl.BlockSpec((d, tf), lambda i, j: (0, j)),
                  pl.BlockSpec((tf, d), lambda i, j: (j, 0))],
        out_specs=pl.BlockSpec((tm, d), lambda i, j: (i, 0)),
        out_shape=jax.ShapeDtypeStruct((m, d), F32),
        scratch_shapes=[pltpu.VMEM((tm, d), BF16)],
        compiler_params=_cparams(("parallel", "arbitrary")),
        name="ffn",
    )(h, g.reshape(1, d), w1, w2)


def _ple_kernel(h_ref, p_ref, g_ref, gf_ref, wg_ref, wp_ref, o_ref, y_ref):
    h = h_ref[...]
    pg = jax.nn.sigmoid(jnp.dot(_rmsnorm(h, g_ref[...]).astype(BF16), wg_ref[...],
                                preferred_element_type=F32))
    e = jnp.dot(p_ref[...].astype(BF16), wp_ref[...], preferred_element_type=F32)
    out = h + e * pg
    o_ref[...] = out
    y_ref[...] = _rmsnorm(out, gf_ref[...])


def _ple(h, ple, g, g_final, wg, wp):
    m, d = h.shape
    tm = 256
    row = lambda w: pl.BlockSpec((tm, w), lambda i: (i, 0))
    full = lambda shape: pl.BlockSpec(shape, lambda i: (0, 0))
    return pl.pallas_call(
        _ple_kernel,
        grid=(m // tm,),
        in_specs=[row(d), row(ple.shape[1]), full((1, d)), full((1, d)), full(wg.shape), full(wp.shape)],
        out_specs=[row(d), row(d)],
        out_shape=[jax.ShapeDtypeStruct((m, d), F32), jax.ShapeDtypeStruct((m, d), F32)],
        compiler_params=_cparams(("parallel",)),
        name="ple",
    )(h, ple, g.reshape(1, d), g_final.reshape(1, d), wg, wp)


def _sortable_key(s):
    b = lax.bitcast_convert_type(s, I32)
    return b ^ ((b >> 31) & 0x7FFFFFFF)


def _lane_sum(acc):
    ones = jnp.ones((LANES, LANES), BF16)
    return jnp.dot(acc.astype(F32).astype(BF16), ones, preferred_element_type=F32).astype(I32)


def _kth_largest_key(count_ge, rows, k):
    zero = jnp.zeros((rows, LANES), I32)
    c0 = count_ge(zero)
    thr = jnp.where(c0 >= k, zero, INT_MIN)
    cnt = jnp.where(c0 >= k, c0, k + 1)

    def pending(cnt):
        return jnp.max(jnp.where(cnt != k, 1, 0))

    def bit_step(it, thr, cnt):
        cand = thr + jnp.left_shift(jnp.int32(1), 30 - it)
        c = count_ge(cand)
        take = c >= k
        return jnp.where(take, cand, thr), jnp.where(take, c, cnt)

    thr, cnt = lax.fori_loop(0, SEARCH_FIXED_BITS, lambda it, s: bit_step(it, *s), (thr, cnt))

    def checked_step(state):
        it, thr, cnt, _ = state
        thr, cnt = bit_step(it, thr, cnt)
        return it + 1, thr, cnt, pending(cnt)

    state = lax.while_loop(lambda s: jnp.logical_and(s[0] < 31, s[3] > 0), checked_step,
                           (jnp.int32(SEARCH_FIXED_BITS), thr, cnt, pending(cnt)))
    return state[1]


def _t5_bucket(rel):
    n = jnp.maximum(rel, 0)
    max_exact = N_BUCKETS // 2
    nf = jnp.maximum(n, 1).astype(F32)
    large = max_exact + (jnp.log(nf / max_exact) / math.log(MAX_DISTANCE / max_exact)
                         * (N_BUCKETS - max_exact)).astype(I32)
    large = jnp.minimum(large, N_BUCKETS - 1)
    return jnp.where(n < max_exact, n, large)


def _bias_of_rel(rel_bias, rel):
    b = jnp.moveaxis(rel_bias[_t5_bucket(rel)], -1, 0).astype(F32)
    return jnp.where((rel >= 0)[None], b, NEG)


def _dsa_prompt_kernel(qb_ref, kb_ref, tid_ref, safe_ref, q_ref, k_ref, v_ref, qi_ref, wi_ref, ki_ref,
                       d_ref, o_ref, keys_ref, thr_ref, m_ref, acc_ref, lg_ref, *, topk, ratio):
    p = pl.program_id(0)
    i = qb_ref[p]
    j = kb_ref[p]
    tq, tk = DSA_TQ, DSA_TK
    jd = i // ratio

    @pl.when(j == 0)
    def _select():
        w = wi_ref[...]
        qpos = i * tq + lax.broadcasted_iota(I32, (tq, tk), 0)
        lane = lax.broadcasted_iota(I32, (tq, tk), 1)

        def chunk(c, carry):
            off = pl.multiple_of(c * tk, tk)
            kic = ki_ref[pl.ds(off, tk), :]
            s = jnp.zeros((tq, tk), F32)
            for h in range(IDX_HEADS):
                s = s + jnp.maximum(_dot_nt(qi_ref[h], kic), 0.0) * w[:, h:h + 1]
            s = jnp.where(off + lane <= qpos, s, NEG)
            keys_ref[c] = _sortable_key(s)
            return carry

        lax.fori_loop(0, jd + 1, chunk, 0)

        def count_ge(cand):
            accs = []
            for r0 in range(0, tq, COUNT_ROWS):
                rs = slice(r0, r0 + COUNT_ROWS)
                cand_rows = cand[rs]

                def body(c, acc, rs=rs, cand_rows=cand_rows):
                    for t in range(tk // LANES):
                        ge = keys_ref[c, rs, t * LANES:(t + 1) * LANES] >= cand_rows
                        acc = acc + ge.astype(I32)
                    return acc

                accs.append(lax.fori_loop(0, jd + 1, body, jnp.zeros((COUNT_ROWS, LANES), I32)))
            return _lane_sum(jnp.concatenate(accs, axis=0))

        thr_ref[...] = _kth_largest_key(count_ge, tq, topk)
        m_ref[...] = jnp.full(m_ref.shape, NEG, F32)
        acc_ref[...] = jnp.zeros(acc_ref.shape, F32)

    thr = jnp.concatenate([thr_ref[...]] * (tk // LANES), axis=1)
    selb = jnp.where(keys_ref[j] >= thr, 0.0, NEG)
    bounded = safe_ref[0] == 1

    @pl.when(bounded)
    def _fixed_offset():
        for h in range(A_HEADS):
            pr = jnp.exp2(_dot_nt(q_ref[h], k_ref[h]) + (selb + d_ref[0, h])).astype(BF16)
            acc_ref[h] += jnp.dot(pr, v_ref[h], preferred_element_type=F32)

    @pl.when(jnp.logical_not(bounded))
    def _running_max():
        block_max = []
        for h in range(A_HEADS):
            lg = _dot_nt(q_ref[h], k_ref[h]) + (selb + d_ref[0, h])
            lg_ref[h] = lg
            block_max.append(jnp.max(lg, axis=1, keepdims=True))
        for h in range(A_HEADS):
            m_prev = m_ref[h]
            m_new = jnp.maximum(m_prev, block_max[h])
            pr = jnp.exp2(lg_ref[h] - m_new).astype(BF16)
            acc_ref[h] = jnp.exp2(m_prev - m_new) * acc_ref[h] + jnp.dot(pr, v_ref[h],
                                                                         preferred_element_type=F32)
            m_ref[h] = m_new

    @pl.when(j == jd)
    def _finish():
        for h in range(A_HEADS):
            acc = acc_ref[h]
            o_ref[h] = acc[:, 0:A_HEAD_DIM] / acc[:, A_HEAD_DIM:A_HEAD_DIM + 1]


def _toeplitz_bias(rel_bias, base, rows, cols):
    n = rows + cols - 2
    rel = jnp.concatenate([base - jnp.arange(cols, dtype=I32),
                           base + rows - 1 - jnp.arange(rows - 1, dtype=I32)])
    z = _bias_of_rel(rel_bias, rel)
    m = jnp.tile(z, (1, rows))[:, :rows * n].reshape(z.shape[0], rows, n)
    return m[:, :, :cols]


def _prompt_bias_tiles(rel_bias):
    tq, tk = DSA_TQ, DSA_TK
    ratio = tk // tq
    tiles = [_toeplitz_bias(rel_bias, a * tq, tq, tk) for a in range(ratio)]
    tiles.append(_toeplitz_bias(rel_bias, tk, tq, tk))
    far = _bias_of_rel(rel_bias, jnp.full((1, 1), MAX_DISTANCE, I32))
    tiles.append(jnp.broadcast_to(far, (far.shape[0], tq, tk)))
    return jnp.stack(tiles) * LOG2E, jnp.max(jnp.abs(rel_bias)) * LOG2E


def _dsa_prompt(q, k, v, qi, wi, ki, bias):
    dtiles, bias_absmax = bias
    t = q.shape[0]
    tq, tk = DSA_TQ, DSA_TK
    ratio = tk // tq
    topk = min(TOPK_MAX, t // 4)
    n_q = t // tq
    n_kc = t // tk
    assert t % tk == 0 and t // LANES <= 256

    qb, kb, tid = [], [], []
    for i in range(n_q):
        jd, a = divmod(i, ratio)
        for j in range(jd + 1):
            qb.append(i)
            kb.append(j)
            tid.append(a if j == jd else (ratio if (j == jd - 1 and a == 0) else ratio + 1))
    qb, kb, tid = (jnp.asarray(np.asarray(x, np.int32)) for x in (qb, kb, tid))

    def heads(x):
        return x.reshape(t, -1, A_HEAD_DIM).transpose(1, 0, 2).astype(BF16)

    qh, kh = heads(q * (A_HEAD_DIM ** -0.5 * LOG2E)), heads(k)
    vh = heads(v)
    vh = jnp.concatenate([vh, jnp.ones((A_HEADS, t, 1), BF16),
                          jnp.zeros((A_HEADS, t, LANES - A_HEAD_DIM - 1), BF16)], axis=-1)
    row_norm = lambda x: jnp.sqrt(jnp.max(jnp.sum(jnp.square(x.astype(F32)), axis=-1), axis=-1))
    bound = jnp.max(row_norm(qh) * row_norm(kh)) + bias_absmax
    safe = (bound <= LOGIT_BOUND).astype(I32).reshape(1)

    grid_spec = pltpu.PrefetchScalarGridSpec(
        num_scalar_prefetch=4,
        grid=(int(qb.shape[0]),),
        in_specs=[
            pl.BlockSpec((A_HEADS, tq, A_HEAD_DIM), lambda p, qb, kb, tid, sf: (0, qb[p], 0)),
            pl.BlockSpec((A_HEADS, tk, A_HEAD_DIM), lambda p, qb, kb, tid, sf: (0, kb[p], 0)),
            pl.BlockSpec((A_HEADS, tk, LANES), lambda p, qb, kb, tid, sf: (0, kb[p], 0)),
            pl.BlockSpec((IDX_HEADS, tq, IDX_DIM), lambda p, qb, kb, tid, sf: (0, qb[p], 0)),
            pl.BlockSpec((tq, IDX_HEADS), lambda p, qb, kb, tid, sf: (qb[p], 0)),
            pl.BlockSpec((t, IDX_DIM), lambda p, qb, kb, tid, sf: (0, 0)),
            pl.BlockSpec((1, A_HEADS, tq, tk), lambda p, qb, kb, tid, sf: (tid[p], 0, 0, 0)),
        ],
        out_specs=pl.BlockSpec((A_HEADS, tq, A_HEAD_DIM), lambda p, qb, kb, tid, sf: (0, qb[p], 0)),
        scratch_shapes=[pltpu.VMEM((n_kc, tq, tk), I32),
                        pltpu.VMEM((tq, LANES), I32),
                        pltpu.VMEM((A_HEADS, tq, 1), F32),
                        pltpu.VMEM((A_HEADS, tq, LANES), F32),
                        pltpu.VMEM((A_HEADS, tq, tk), F32)],
    )
    out = pl.pallas_call(
        functools.partial(_dsa_prompt_kernel, topk=topk, ratio=ratio),
        grid_spec=grid_spec,
        out_shape=jax.ShapeDtypeStruct((A_HEADS, t, A_HEAD_DIM), F32),
        compiler_params=_cparams(("arbitrary",)),
        name="dsa_prompt",
    )(qb, kb, tid, safe, qh, kh, vh, heads(qi), wi, ki.astype(BF16), dtiles)
    return out.transpose(1, 0, 2).reshape(t, A_WIDTH)


def _dsa_sample_select_kernel(pt_ref, qi_ref, wi_ref, *rest, n_pages, pps, n_new, topk):
    cki_refs = rest[:pps]
    kin_ref, keys_ref, thr_ref = rest[pps:]
    p = pl.program_id(1)
    rows = n_new
    page = kin_ref.shape[2]
    w = wi_ref[0][:, 0:1]

    def scores(kt):
        r = jnp.maximum(jnp.dot(qi_ref[0], kt.astype(BF16), preferred_element_type=F32), 0.0) * w
        return jnp.sum(r.reshape(IDX_HEADS, rows, r.shape[-1]), axis=0)

    s = scores(jnp.concatenate([r[0, 0] for r in cki_refs], axis=1))
    for r in range(pps):
        keys_ref[0, p * pps + r] = _sortable_key(s[:, r * page:(r + 1) * page])

    @pl.when(p == n_pages // pps - 1)
    def _new():
        s = scores(kin_ref[0])
        qrow = lax.broadcasted_iota(I32, s.shape, 0)
        col = lax.broadcasted_iota(I32, s.shape, 1)
        keys_ref[0, n_pages] = _sortable_key(jnp.where(col <= qrow, s, NEG))

        def count_ge(cand):
            parts = [(keys_ref[0, c] >= cand).astype(I32) for c in range(n_pages + 1)]
            while len(parts) > 1:
                parts = [a + b for a, b in zip(parts[::2], parts[1::2])] + parts[len(parts) & ~1:]
            return _lane_sum(parts[0])

        thr_ref[0] = _kth_largest_key(count_ge, rows, topk)


def _dsa_sample_attend_kernel(pt_ref, q_ref, keys_ref, thr_ref, *rest, n_pages, pps):
    ck_refs, cv_refs = rest[:pps], rest[pps:2 * pps]
    (kn_ref, vn_ref, dfar_ref, dlast_ref, dnew_ref, o_ref, m_ref, l_ref, acc_ref) = rest[2 * pps:]
    p = pl.program_id(1)
    last = p == n_pages // pps - 1
    rows = keys_ref.shape[2]
    hd = A_HEADS * A_HEAD_DIM

    @pl.when(p == 0)
    def _init():
        m_ref[...] = jnp.full(m_ref.shape, NEG, F32)
        l_ref[...] = jnp.zeros(l_ref.shape, F32)
        acc_ref[...] = jnp.zeros(acc_ref.shape, F32)

    thr = thr_ref[0]

    def attend(kt, vt, key_pages, dtile):
        selb = jnp.concatenate([jnp.where(kp >= thr, 0.0, NEG) for kp in key_pages], axis=1)
        selb = jnp.concatenate([selb] * A_HEADS, axis=0)
        lg = jnp.dot(q_ref[0], kt, preferred_element_type=F32) + (selb + dtile)
        m_prev = m_ref[...]
        m_new = jnp.maximum(m_prev, jnp.max(lg, axis=1, keepdims=True))
        alpha = jnp.exp2(m_prev - m_new)
        pr = jnp.exp2(lg - m_new)
        l_ref[...] = alpha * l_ref[...] + jnp.sum(pr, axis=1, keepdims=True)
        acc_ref[...] = alpha * acc_ref[...] + _dot_nt(pr.astype(BF16), vt)
        m_ref[...] = m_new

    kt = jnp.concatenate([r[0, 0].reshape(hd, -1).astype(BF16) for r in ck_refs], axis=1)
    vt = jnp.concatenate([r[0, 0].reshape(hd, -1).astype(BF16) for r in cv_refs], axis=1)
    attend(kt, vt, [keys_ref[0, p * pps + r] for r in range(pps)],
           jnp.where(last, dlast_ref[...], dfar_ref[...]))

    @pl.when(last)
    def _new():
        attend(kn_ref[0].astype(BF16), vn_ref[0].astype(BF16), [keys_ref[0, n_pages]], dnew_ref[...])
        acc = acc_ref[...]
        l = l_ref[...]
        for h in range(A_HEADS):
            rs = slice(h * rows, (h + 1) * rows)
            o_ref[0, rs, :] = acc[rs, h * A_HEAD_DIM:(h + 1) * A_HEAD_DIM] / l[rs]


def _sample_bias_tiles(rel_bias, n_new, page):
    pps = SAMPLE_PAGES_PER_STEP
    qq = jnp.arange(n_new, dtype=I32)[:, None]
    pos = jnp.arange(page, dtype=I32)[None, :]
    tile = lambda rel: _bias_of_rel(rel_bias, rel).reshape(A_HEADS * n_new, page) * LOG2E
    far = tile(jnp.full((n_new, page), MAX_DISTANCE, I32))
    last = tile(page + qq - pos)
    new = tile(qq - pos)
    return jnp.tile(far, (1, pps)), jnp.concatenate([far] * (pps - 1) + [last], axis=1), new


def _dsa_sample(layer, q, k_new, v_new, qi, wi, ki_new, cache_k, cache_v, cache_idx_k,
                page_table, dtiles):
    bd, n_pages = page_table.shape
    page = cache_k.shape[2]
    n_new = q.shape[0] // bd
    past = n_pages * page
    topk = min(TOPK_MAX, (past + n_new) // 4)
    hrows = A_HEADS * n_new
    pcols = page * A_HEADS
    pps = SAMPLE_PAGES_PER_STEP
    n_steps = n_pages // pps
    assert n_pages % pps == 0 and n_pages + 1 <= 256
    pt = page_table.reshape(-1).astype(I32)

    def head_rows(x):
        return x.reshape(bd, n_new, -1, A_HEAD_DIM).transpose(0, 2, 1, 3).reshape(bd, hrows, A_HEAD_DIM)

    qi2 = head_rows(qi).astype(BF16)
    wi2 = jnp.broadcast_to(wi.reshape(bd, n_new, IDX_HEADS).transpose(0, 2, 1).reshape(bd, hrows, 1),
                           (bd, hrows, LANES))

    def lanes_last(x):
        x = x.reshape(bd, n_new, -1).transpose(0, 2, 1)
        return jnp.pad(x, ((0, 0), (0, 0), (0, page - n_new)))

    cki_t = cache_idx_k.transpose(0, 1, 3, 2)
    ck_t = cache_k.transpose(0, 1, 3, 4, 2)
    cv_t = cache_v.transpose(0, 1, 3, 4, 2)

    def page_spec(block, r):
        zeros = (0,) * (len(block) - 2)
        return pl.BlockSpec(block, lambda b, p, pt: (layer, pt[b * n_pages + p * pps + r]) + zeros)

    per_batch = lambda block: pl.BlockSpec(block, lambda b, p, pt: (b,) + (0,) * (len(block) - 1))
    const = lambda block: pl.BlockSpec(block, lambda b, p, pt: (0,) * len(block))

    keys, thr = pl.pallas_call(
        functools.partial(_dsa_sample_select_kernel, n_pages=n_pages, pps=pps, n_new=n_new, topk=topk),
        grid_spec=pltpu.PrefetchScalarGridSpec(
            num_scalar_prefetch=1,
            grid=(bd, n_steps),
            in_specs=([per_batch((1, hrows, IDX_DIM)), per_batch((1, hrows, LANES))]
                      + [page_spec((1, 1, IDX_DIM, page), r) for r in range(pps)]
                      + [per_batch((1, IDX_DIM, page))]),
            out_specs=[per_batch((1, n_pages + 1, n_new, page)), per_batch((1, n_new, LANES))],
        ),
        out_shape=[jax.ShapeDtypeStruct((bd, n_pages + 1, n_new, page), I32),
                   jax.ShapeDtypeStruct((bd, n_new, LANES), I32)],
        compiler_params=_cparams(("parallel", "arbitrary")),
        name="dsa_sample_select",
    )(pt, qi2, wi2, *([cki_t] * pps), lanes_last(ki_new))

    dfar, dlast, dnew = dtiles
    q4 = (q * (A_HEAD_DIM ** -0.5 * LOG2E)).reshape(bd, n_new, A_HEADS, A_HEAD_DIM)
    qbd = jnp.einsum("bqhd,hg->bhqgd", q4, jnp.eye(A_HEADS, dtype=F32)).reshape(bd, hrows, A_WIDTH)

    cache_block = (1, 1, A_HEADS, A_HEAD_DIM, page)
    out = pl.pallas_call(
        functools.partial(_dsa_sample_attend_kernel, n_pages=n_pages, pps=pps),
        grid_spec=pltpu.PrefetchScalarGridSpec(
            num_scalar_prefetch=1,
            grid=(bd, n_steps),
            in_specs=([per_batch((1, hrows, A_WIDTH)), per_batch((1, n_pages + 1, n_new, page)),
                       per_batch((1, n_new, LANES))]
                      + [page_spec(cache_block, r) for r in range(pps)]
                      + [page_spec(cache_block, r) for r in range(pps)]
                      + [per_batch((1, A_WIDTH, page)), per_batch((1, A_WIDTH, page)),
                         const((hrows, pps * page)), const((hrows, pps * page)), const((hrows, page))]),
            out_specs=per_batch((1, hrows, A_HEAD_DIM)),
            scratch_shapes=[pltpu.VMEM((hrows, 1), F32), pltpu.VMEM((hrows, 1), F32),
                            pltpu.VMEM((hrows, A_WIDTH), F32)],
        ),
        out_shape=jax.ShapeDtypeStruct((bd, hrows, A_HEAD_DIM), F32),
        compiler_params=_cparams(("parallel", "arbitrary")),
        name="dsa_sample_attend",
    )(pt, qbd.astype(BF16), keys, thr, *([ck_t] * pps), *([cv_t] * pps),
      lanes_last(k_new), lanes_last(v_new), dfar, dlast, dnew)
    return out.reshape(bd, A_HEADS, n_new, A_HEAD_DIM).transpose(0, 2, 1, 3).reshape(bd * n_new, A_WIDTH)


def _s5_in_kernel(u_ref, bb_ref, o_ref):
    o_ref[...] = _dot_exact(u_ref[...], bb_ref[...])


def _s5_scan_kernel(bu_ref, s0_ref, ab_ref, o_ref, fin_ref, st_ref, *, tc):
    @pl.when(pl.program_id(1) == 0)
    def _():
        st_ref[...] = s0_ref[0]

    ar, ai = ab_ref[0], ab_ref[1]

    def step(t, carry):
        sr, si = carry
        nr = ar * sr - ai * si + bu_ref[0, t, 0]
        ni = ar * si + ai * sr + bu_ref[0, t, 1]
        o_ref[0, t, 0] = nr
        o_ref[0, t, 1] = ni
        return nr, ni

    sr, si = lax.fori_loop(0, tc, step, (st_ref[0], st_ref[1]), unroll=8)
    st_ref[0] = sr
    st_ref[1] = si
    fin_ref[0, 0] = sr
    fin_ref[0, 1] = si


def _s5_out_kernel(s_ref, u_ref, c_ref, d_ref, wg_ref, o_ref):
    u = u_ref[...]
    y = _dot_exact(s_ref[...], c_ref[...]) + u * d_ref[...]
    hdn = jax.nn.gelu(y)
    gate = jax.nn.sigmoid(jnp.dot(hdn.astype(BF16), wg_ref[...], preferred_element_type=F32))
    o_ref[...] = hdn * gate


def _s5(u, s0_re, s0_im, lam_re, lam_im, log_step, b_re, b_im, c_re, c_im, d, w_glu, batch):
    m = u.shape[0]
    t = m // batch
    g, p = S5_GROUPS, S5_STATE
    n_state = g * p
    lam_re = jnp.minimum(lam_re.astype(F32), -1e-4)
    lam_im = lam_im.astype(F32)
    dt = jnp.exp(log_step.astype(F32))[:, None]
    mag = jnp.exp(lam_re * dt)
    ab_re, ab_im = mag * jnp.cos(lam_im * dt), mag * jnp.sin(lam_im * dt)
    den = lam_re * lam_re + lam_im * lam_im
    nr, ni = ab_re - 1.0, ab_im
    f_re = (nr * lam_re + ni * lam_im) / den
    f_im = (ni * lam_re - nr * lam_im) / den
    bb_re = f_re[..., None] * b_re - f_im[..., None] * b_im
    bb_im = f_re[..., None] * b_im + f_im[..., None] * b_re
    eye = jnp.eye(g, dtype=F32)
    blk_in = lambda x: jnp.einsum("gph,gk->ghkp", x, eye).reshape(S5_WIDTH, n_state)
    blk_out = lambda x: jnp.einsum("ghp,gk->kpgh", x, eye).reshape(n_state, S5_WIDTH)
    bb = jnp.concatenate([blk_in(bb_re), blk_in(bb_im)], axis=1)
    cc = jnp.concatenate([blk_out(c_re.astype(F32)), -blk_out(c_im.astype(F32))], axis=0)
    ab = jnp.stack([ab_re.reshape(SUBLANES, LANES), ab_im.reshape(SUBLANES, LANES)])
    s0 = jnp.stack([s0_re.reshape(batch, SUBLANES, LANES), s0_im.reshape(batch, SUBLANES, LANES)], axis=1)

    tm = 256
    bu = pl.pallas_call(
        _s5_in_kernel,
        grid=(m // tm,),
        in_specs=[pl.BlockSpec((tm, S5_WIDTH), lambda i: (i, 0)),
                  pl.BlockSpec(bb.shape, lambda i: (0, 0))],
        out_specs=pl.BlockSpec((tm, 2 * n_state), lambda i: (i, 0)),
        out_shape=jax.ShapeDtypeStruct((m, 2 * n_state), F32),
        compiler_params=_cparams(("parallel",)),
        name="s5_in",
    )(u, bb)

    tc = min(t, 256)
    tile = (1, tc, 2, SUBLANES, LANES)
    states, fin = pl.pallas_call(
        functools.partial(_s5_scan_kernel, tc=tc),
        grid=(batch, t // tc),
        in_specs=[pl.BlockSpec(tile, lambda b, c: (b, c, 0, 0, 0)),
                  pl.BlockSpec((1, 2, SUBLANES, LANES), lambda b, c: (b, 0, 0, 0)),
                  pl.BlockSpec((2, SUBLANES, LANES), lambda b, c: (0, 0, 0))],
        out_specs=[pl.BlockSpec(tile, lambda b, c: (b, c, 0, 0, 0)),
                   pl.BlockSpec((1, 2, SUBLANES, LANES), lambda b, c: (b, 0, 0, 0))],
        out_shape=[jax.ShapeDtypeStruct((batch, t, 2, SUBLANES, LANES), F32),
                   jax.ShapeDtypeStruct((batch, 2, SUBLANES, LANES), F32)],
        scratch_shapes=[pltpu.VMEM((2, SUBLANES, LANES), F32)],
        compiler_params=_cparams(("parallel", "arbitrary")),
        name="s5_scan",
    )(bu.reshape(batch, t, 2, SUBLANES, LANES), s0, ab)

    out = pl.pallas_call(
        _s5_out_kernel,
        grid=(m // tm,),
        in_specs=[pl.BlockSpec((tm, 2 * n_state), lambda i: (i, 0)),
                  pl.BlockSpec((tm, S5_WIDTH), lambda i: (i, 0)),
                  pl.BlockSpec(cc.shape, lambda i: (0, 0)),
                  pl.BlockSpec((1, S5_WIDTH), lambda i: (0, 0)),
                  pl.BlockSpec((S5_WIDTH, S5_WIDTH), lambda i: (0, 0))],
        out_specs=pl.BlockSpec((tm, S5_WIDTH), lambda i: (i, 0)),
        out_shape=jax.ShapeDtypeStruct((m, S5_WIDTH), F32),
        compiler_params=_cparams(("parallel",)),
        name="s5_out",
    )(states.reshape(m, 2 * n_state), u, cc, d.reshape(1, S5_WIDTH).astype(F32), w_glu.astype(BF16))
    return out, fin[:, 0].reshape(batch, g, p), fin[:, 1].reshape(batch, g, p)


_RW_OPS = ("r", "k", "v", "d", "kk", "ka")


def _head_sum_matrix():
    hid = np.arange(RW_WIDTH) // RW_HEAD_DIM
    return jnp.asarray((hid[:, None] == hid[None, :]).astype(np.float32))


def _rwkv_pre_kernel(*refs, with_vres):
    if with_vres:
        (pc_ref, prev_ref, vf_ref, mu_ref, w0_ref, w2_ref, a0_ref, a2_ref, g2_ref, kk_ref, ka_ref,
         rk_ref, j_ref, v0_ref, v1_ref, v2_ref, ops_ref, gb_ref) = refs
    else:
        (pc_ref, prev_ref, mu_ref, w0_ref, w2_ref, a0_ref, a2_ref, g2_ref, kk_ref, ka_ref,
         rk_ref, j_ref, ops_ref, gb_ref) = refs
    pc = pc_ref[...]
    xm = pc + (prev_ref[...] - pc) * mu_ref[...]
    w_ = RW_WIDTH
    r, k, v = xm[:, 0:w_], xm[:, w_:2 * w_], xm[:, 2 * w_:3 * w_]
    o = 3 * w_
    lora = xm[:, o:o + RW_DECAY_LORA + RW_AAA_LORA]
    gl = xm[:, o + RW_DECAY_LORA + RW_AAA_LORA:]
    mm = lambda a, b_ref: jnp.dot(a.astype(BF16), b_ref[...].astype(BF16), preferred_element_type=F32)
    z = -(w0_ref[...] + mm(jnp.tanh(lora), w2_ref))
    softplus = jnp.maximum(z, 0.0) + jnp.log(1.0 + jnp.exp(-jnp.abs(z)))
    w = -softplus - 0.5
    decay = jnp.exp(-jnp.exp(w))
    a = jax.nn.sigmoid(a0_ref[...] + mm(lora, a2_ref))
    g = mm(jax.nn.sigmoid(gl), g2_ref)
    if with_vres:
        v = v + (vf_ref[...] - v) * jax.nn.sigmoid(v0_ref[...] + mm(mm(v, v1_ref), v2_ref))
    jm = j_ref[...]
    kk = k * kk_ref[...]
    kk = kk * lax.rsqrt(jnp.maximum(_dot_exact(kk * kk, jm), 1e-24))
    k = k * (1.0 + (a - 1.0) * ka_ref[...])
    bonus = _dot_exact(r * k * rk_ref[...], jm) * v
    for idx, val in enumerate((r, k, v, decay, kk, kk * a)):
        ops_ref[:, idx * w_:(idx + 1) * w_] = val
    gb_ref[:, 0:w_] = g
    gb_ref[:, w_:2 * w_] = bonus


def _rwkv_scan_kernel(ops_ref, s0_ref, mask_ref, j_ref, y_ref, fin_ref, st_ref, zh_ref, zl_ref, e_ref,
                      *, tc, sub):
    @pl.when(pl.program_id(1) == 0)
    def _():
        st_ref[...] = s0_ref[0]

    n, w_ = RW_HEAD_DIM, RW_WIDTH
    col = {name: idx * w_ for idx, name in enumerate(_RW_OPS)}
    expand = ("d", "kk", "ka", "k", "r")
    mask = mask_ref[...] > 0.5
    jm = j_ref[...]

    def sub_chunk(sc, state):
        t0 = pl.multiple_of(sc * sub, sub)
        for oi, name in enumerate(expand):
            two_terms = name == "d"
            for t in range(sub):
                row = ops_ref[0, pl.ds(t0 + t, 1), col[name]:col[name] + w_]
                hi = row.astype(BF16).astype(F32)
                zh_ref[t * n:(t + 1) * n, :] = jnp.where(mask, jnp.broadcast_to(hi, (n, w_)), 0.0).astype(BF16)
                if two_terms:
                    lo = row - hi
                    zl_ref[t * n:(t + 1) * n, :] = jnp.where(mask, jnp.broadcast_to(lo, (n, w_)), 0.0).astype(BF16)
            e = jnp.dot(zh_ref[...], jm, preferred_element_type=F32)
            if two_terms:
                e = e + jnp.dot(zl_ref[...], jm, preferred_element_type=F32)
            e_ref[oi] = e
        for t in range(sub):
            rows = slice(t * n, (t + 1) * n)
            vrow = ops_ref[0, pl.ds(t0 + t, 1), col["v"]:col["v"] + w_]
            sa = -jnp.sum(state * e_ref[1, rows, :], axis=0, keepdims=True)
            state = e_ref[0, rows, :] * state + e_ref[2, rows, :] * sa + e_ref[3, rows, :] * vrow
            y_ref[0, pl.ds(t0 + t, 1), :] = jnp.sum(state * e_ref[4, rows, :], axis=0, keepdims=True)
        return state

    state = lax.fori_loop(0, tc // sub, sub_chunk, st_ref[...])
    st_ref[...] = state
    fin_ref[0] = state


def _rwkv_post_kernel(y_ref, gb_ref, lw_ref, lb_ref, j_ref, o_ref):
    y = y_ref[...]
    jm = j_ref[...]
    inv = 1.0 / RW_HEAD_DIM
    mean = _dot_exact(y, jm) * inv
    yc = y - mean
    var = _dot_exact(yc * yc, jm) * inv
    yn = yc * lax.rsqrt(var + RW_GN_EPS) * lw_ref[...] + lb_ref[...]
    gb = gb_ref[...]
    o_ref[...] = (yn + gb[:, RW_WIDTH:]) * gb[:, 0:RW_WIDTH]


def _rwkv(pc, shift_prev, wkv0, v_first, vres, mu, w0, w2, a0, a2, g2, k_k, k_a, r_k, ln_w, ln_b, batch):
    m = pc.shape[0]
    t = m // batch
    w_ = RW_WIDTH
    pc3 = pc.reshape(batch, t, RW_PROJ)
    prev = jnp.concatenate([shift_prev.astype(F32)[:, None], pc3[:, :-1]], axis=1).reshape(m, RW_PROJ)
    jm = _head_sum_matrix()
    row = lambda x: x.reshape(1, -1).astype(F32)
    tm = 256
    rspec = lambda width: pl.BlockSpec((tm, width), lambda i: (i, 0))
    fspec = lambda a: pl.BlockSpec(a.shape, lambda i: (0,) * a.ndim)
    w2p = jnp.pad(w2, ((0, RW_AAA_LORA), (0, 0)))
    a2p = jnp.pad(a2, ((RW_DECAY_LORA, 0), (0, 0)))
    params = [row(mu), row(w0), w2p, row(a0), a2p, g2, row(k_k), row(k_a), row(r_k), jm]
    ins = [pc, prev]
    specs = [rspec(RW_PROJ), rspec(RW_PROJ)]
    if vres is not None:
        ins.append(v_first)
        specs.append(rspec(w_))
        params += [row(vres[0]), vres[1], vres[2]]
    ops, gb = pl.pallas_call(
        functools.partial(_rwkv_pre_kernel, with_vres=vres is not None),
        grid=(m // tm,),
        in_specs=specs + [fspec(a) for a in params],
        out_specs=[rspec(len(_RW_OPS) * w_), rspec(2 * w_)],
        out_shape=[jax.ShapeDtypeStruct((m, len(_RW_OPS) * w_), F32),
                   jax.ShapeDtypeStruct((m, 2 * w_), F32)],
        compiler_params=_cparams(("parallel",)),
        name="rwkv_pre",
    )(*ins, *params)
    if vres is None:
        v_first = ops[:, 2 * w_:3 * w_]

    n = RW_HEAD_DIM
    sub = 8
    tc = min(t, 64)
    jidx = np.arange(w_) % n
    mask = jnp.asarray((np.arange(n)[:, None] == jidx[None, :]).astype(np.float32))
    p0 = wkv0.astype(F32).transpose(0, 3, 1, 2).reshape(batch, n, w_)
    y, fin = pl.pallas_call(
        functools.partial(_rwkv_scan_kernel, tc=tc, sub=sub),
        grid=(batch, t // tc),
        in_specs=[pl.BlockSpec((1, tc, len(_RW_OPS) * w_), lambda b, c: (b, c, 0)),
                  pl.BlockSpec((1, n, w_), lambda b, c: (b, 0, 0)),
                  pl.BlockSpec((n, w_), lambda b, c: (0, 0)),
                  pl.BlockSpec((w_, w_), lambda b, c: (0, 0))],
        out_specs=[pl.BlockSpec((1, tc, w_), lambda b, c: (b, c, 0)),
                   pl.BlockSpec((1, n, w_), lambda b, c: (b, 0, 0))],
        out_shape=[jax.ShapeDtypeStruct((batch, t, w_), F32),
                   jax.ShapeDtypeStruct((batch, n, w_), F32)],
        scratch_shapes=[pltpu.VMEM((n, w_), F32),
                        pltpu.VMEM((sub * n, w_), BF16),
                        pltpu.VMEM((sub * n, w_), BF16),
                        pltpu.VMEM((5, sub * n, w_), F32)],
        compiler_params=_cparams(("parallel", "arbitrary")),
        name="rwkv_scan",
    )(ops.reshape(batch, t, len(_RW_OPS) * w_), p0, mask, jm.astype(BF16))
    wkv = fin.reshape(batch, n, RW_HEADS, n).transpose(0, 2, 3, 1)

    out = pl.pallas_call(
        _rwkv_post_kernel,
        grid=(m // tm,),
        in_specs=[rspec(w_), rspec(2 * w_), fspec(row(ln_w)), fspec(row(ln_b)), fspec(jm)],
        out_specs=rspec(w_),
        out_shape=jax.ShapeDtypeStruct((m, w_), F32),
        compiler_params=_cparams(("parallel",)),
        name="rwkv_post",
    )(y.reshape(m, w_), gb, row(ln_w), row(ln_b), jm)
    return out, wkv, v_first


def _relayout_w_in(w_in):
    cols = []
    for name, width in _PIECES:
        src, _, _ = _LAYOUT[name]
        piece = w_in[:, src:src + width]
        pad = -(-width // LANES) * LANES - width
        cols.append(jnp.pad(piece, ((0, 0), (0, pad))) if pad else piece)
    return jnp.concatenate(cols, axis=1).astype(BF16)


def _proj_scale_row():
    s = np.ones((1, PROJ_COLS), np.float32)
    _, d0, w0 = _LAYOUT["qi"]
    s[0, d0:d0 + w0] = IDX_DIM ** -0.5
    _, d1, w1 = _LAYOUT["wi"]
    s[0, d1:d1 + w1] = IDX_HEADS ** -0.5
    return jnp.asarray(s)


def _trunk(x, ple, attend, s5_re0, s5_im0, wkv0, shift0, W):
    batch, t, _ = x.shape
    m = batch * t
    depth = W["w_in"].shape[0]
    h = x.reshape(m, D_MODEL)
    v_first = None
    ks, vs, kis, sres, sims, wkvs, shifts = [], [], [], [], [], [], []
    scale_row = _proj_scale_row()
    ones_row = jnp.ones((1, 3 * D_MODEL), F32)
    y = None

    def piece(proj, name):
        _, dst, width = _LAYOUT[name]
        return proj[:, dst:dst + width]

    for i in range(depth):
        proj = _norm_mm(h, W["norm_mix"][i], _relayout_w_in(W["w_in"][i]), scale_row)
        gates = _norm_mm(h, W["norm_mix"][i], W["w_gate"][i].astype(BF16), ones_row, act="sigmoid")
        q, k, v, qi, wi, ki, u, pc = (piece(proj, n) for n, _ in _PIECES)
        o_a = attend(i, q, k, v, qi, wi, ki)
        o_b, s_re, s_im = _s5(u, s5_re0[i], s5_im0[i], W["ssm_lambda_re"][i], W["ssm_lambda_im"][i],
                              W["ssm_log_step"][i], W["ssm_b_re"][i], W["ssm_b_im"][i],
                              W["ssm_c_re"][i], W["ssm_c_im"][i], W["ssm_d"][i], W["ssm_w_glu"][i], batch)
        vres = None if i == 0 else (W["rw_v0"][i - 1], W["rw_v1"][i - 1], W["rw_v2"][i - 1])
        o_c, wkv, v_first = _rwkv(pc, shift0[i], wkv0[i], v_first, vres, W["rw_mu"][i], W["rw_w0"][i],
                                  W["rw_w2"][i], W["rw_a0"][i], W["rw_a2"][i], W["rw_g2"][i],
                                  W["rw_k_k"][i], W["rw_k_a"][i], W["rw_r_k"][i], W["rw_ln_w"][i],
                                  W["rw_ln_b"][i], batch)
        h = _merge(o_a, o_b, o_c, gates, h, W["w_up_a"][i].astype(BF16), W["w_up_b"][i].astype(BF16),
                   W["w_up_c"][i].astype(BF16), W["w_out"][i].astype(BF16))
        h = _ffn(h, W["norm_ffn"][i], W["w_ff1"][i].astype(BF16), W["w_ff2"][i].astype(BF16))
        h, y = _ple(h, ple[i].reshape(m, -1), W["norm_ple"][i], W["norm_final"],
                    W["w_ple_gate"][i].astype(BF16), W["w_ple"][i].astype(BF16))
        ks.append(k.reshape(batch, t, A_HEADS, A_HEAD_DIM))
        vs.append(v.reshape(batch, t, A_HEADS, A_HEAD_DIM))
        kis.append(ki.reshape(batch, t, IDX_DIM))
        sres.append(s_re)
        sims.append(s_im)
        wkvs.append(wkv)
        shifts.append(pc.reshape(batch, t, RW_PROJ)[:, -1])
    return (y.reshape(batch, t, D_MODEL), jnp.stack(ks), jnp.stack(vs), jnp.stack(kis), jnp.stack(sres),
            jnp.stack(sims), jnp.stack(wkvs), jnp.stack(shifts))


def kernel(x_prompt, x_sample, cache_k, cache_v, cache_idx_k, state_ssm_re, state_ssm_im, state_wkv,
           state_shift, page_table, p_prompt, p_sample, rel_bias, norm_mix, norm_ffn, norm_ple, norm_final,
           w_in, w_up_a, w_up_b, w_up_c, w_gate, w_out, ssm_lambda_re, ssm_lambda_im, ssm_log_step,
           ssm_b_re, ssm_b_im, ssm_c_re, ssm_c_im, ssm_d, ssm_w_glu, rw_mu, rw_w0, rw_w2, rw_a0, rw_a2,
           rw_g2, rw_k_k, rw_k_a, rw_r_k, rw_ln_w, rw_ln_b, rw_v0, rw_v1, rw_v2, w_ff1, w_ff2,
           w_ple, w_ple_gate):
    W = dict(norm_mix=norm_mix, norm_ffn=norm_ffn, norm_ple=norm_ple, norm_final=norm_final, w_in=w_in,
             w_up_a=w_up_a, w_up_b=w_up_b, w_up_c=w_up_c, w_gate=w_gate, w_out=w_out,
             ssm_lambda_re=ssm_lambda_re, ssm_lambda_im=ssm_lambda_im, ssm_log_step=ssm_log_step,
             ssm_b_re=ssm_b_re, ssm_b_im=ssm_b_im, ssm_c_re=ssm_c_re, ssm_c_im=ssm_c_im, ssm_d=ssm_d,
             ssm_w_glu=ssm_w_glu, rw_mu=rw_mu, rw_w0=rw_w0, rw_w2=rw_w2, rw_a0=rw_a0, rw_a2=rw_a2,
             rw_g2=rw_g2, rw_k_k=rw_k_k, rw_k_a=rw_k_a, rw_r_k=rw_r_k, rw_ln_w=rw_ln_w, rw_ln_b=rw_ln_b,
             rw_v0=rw_v0, rw_v1=rw_v1, rw_v2=rw_v2, w_ff1=w_ff1, w_ff2=w_ff2, w_ple=w_ple,
             w_ple_gate=w_ple_gate)
    depth = w_in.shape[0]
    bp = x_prompt.shape[0]

    prompt_tiles = _prompt_bias_tiles(rel_bias)
    sample_tiles = _sample_bias_tiles(rel_bias, x_sample.shape[1], cache_k.shape[2])

    def attend_prompt(i, q, k, v, qi, wi, ki):
        return _dsa_prompt(q, k, v, qi, wi, ki, prompt_tiles)

    def attend_sample(i, q, k, v, qi, wi, ki):
        return _dsa_sample(i, q, k, v, qi, wi, ki, cache_k, cache_v, cache_idx_k, page_table, sample_tiles)

    zs = jnp.zeros((depth, bp, S5_GROUPS, S5_STATE), F32)
    zw = jnp.zeros((depth, bp, RW_HEADS, RW_HEAD_DIM, RW_HEAD_DIM), F32)
    zsh = jnp.zeros((depth, bp, RW_PROJ), x_prompt.dtype)
    (y_prompt, k_prompt, v_prompt, idx_k_prompt, ssm_re_prompt, ssm_im_prompt, wkv_prompt,
     shift_prompt) = _trunk(x_prompt, p_prompt, attend_prompt, zs, zs, zw, zsh, W)
    (y_sample, k_sample, v_sample, idx_k_sample, ssm_re_sample, ssm_im_sample, wkv_sample,
     shift_sample) = _trunk(x_sample, p_sample, attend_sample, state_ssm_re, state_ssm_im, state_wkv,
                            state_shift, W)
    return (y_prompt, y_sample, k_prompt, v_prompt, idx_k_prompt, k_sample, v_sample, idx_k_sample,
            ssm_re_prompt, ssm_im_prompt, ssm_re_sample, ssm_im_sample, wkv_prompt, wkv_sample,
            shift_prompt, shift_sample)
```

```python
import functools
import math

import numpy as np
import jax
import jax.numpy as jnp
from jax import lax
from jax.experimental import pallas as pl
from jax.experimental.pallas import tpu as pltpu

F32 = jnp.float32
BF16 = jnp.bfloat16
I32 = jnp.int32

D_MODEL = 1024
A_HEADS = 8
A_HEAD_DIM = 64
A_WIDTH = A_HEADS * A_HEAD_DIM
IDX_HEADS = 8
IDX_DIM = 64
TOPK_MAX = 256
N_BUCKETS = 32
MAX_DISTANCE = 128
S5_GROUP = 16
S5_GROUPS = 16
S5_WIDTH = 256
S5_STATE = 64
RW_HEADS = 4
RW_HEAD_DIM = 64
RW_WIDTH = 256
RW_DECAY_LORA = 64
RW_AAA_LORA = 64
RW_GATE_LORA = 128
RW_PROJ = 1024
RW_GN_EPS = 64e-5
D_FF = 4 * D_MODEL
EPS = 1e-6
NEG = -1e30
INT_MIN = -(2 ** 31)
LOG2E = math.log2(math.e)
LOGIT_BOUND = 60.0

LANES = 128
SUBLANES = 8
VMEM_LIMIT = 56 * 1024 * 1024

_PIECES = (("q", A_WIDTH), ("k", A_WIDTH), ("v", A_WIDTH), ("qi", IDX_HEADS * IDX_DIM),
           ("wi", IDX_HEADS), ("ki", IDX_DIM), ("u", S5_WIDTH), ("pc", RW_PROJ))


def _piece_layout():
    src, dst, out = 0, 0, {}
    for name, width in _PIECES:
        padded = -(-width // LANES) * LANES
        out[name] = (src, dst, width)
        src += width
        dst += padded
    return out, dst


_LAYOUT, PROJ_COLS = _piece_layout()

DSA_TQ = 256
DSA_TK = 512
SAMPLE_PAGES_PER_STEP = 8
VT_ROWS = 80
COUNT_ROWS = 128
SEARCH_FIXED_BITS = 18


def _cparams(sem):
    return pltpu.CompilerParams(dimension_semantics=sem, vmem_limit_bytes=VMEM_LIMIT)


def _rmsnorm(x, g):
    return x * lax.rsqrt(jnp.mean(x * x, axis=-1, keepdims=True) + EPS) * g


def _dot_exact(a, b):
    return jnp.dot(a, b, precision=lax.Precision.HIGHEST, preferred_element_type=F32)


def _dot_nt(a, b):
    return lax.dot_general(a, b, (((1,), (1,)), ((), ())), preferred_element_type=F32)


def _row_tile(m):
    return 512 if m % 512 == 0 else 256


def _norm_mm_kernel(x_ref, g_ref, w_ref, s_ref, o_ref, xn_ref, *, act):
    @pl.when(pl.program_id(1) == 0)
    def _():
        xn_ref[...] = _rmsnorm(x_ref[...], g_ref[...]).astype(BF16)

    y = jnp.dot(xn_ref[...], w_ref[...], preferred_element_type=F32)
    if act == "sigmoid":
        y = jax.nn.sigmoid(y)
    o_ref[...] = y * s_ref[...]


def _norm_mm(x, g, w_bf16, scale_row, act=None, tn=512):
    m, d = x.shape
    n = w_bf16.shape[1]
    tm = _row_tile(m)
    return pl.pallas_call(
        functools.partial(_norm_mm_kernel, act=act),
        grid=(m // tm, n // tn),
        in_specs=[pl.BlockSpec((tm, d), lambda i, j: (i, 0)),
                  pl.BlockSpec((1, d), lambda i, j: (0, 0)),
                  pl.BlockSpec((d, tn), lambda i, j: (0, j)),
                  pl.BlockSpec((1, tn), lambda i, j: (0, j))],
        out_specs=pl.BlockSpec((tm, tn), lambda i, j: (i, j)),
        out_shape=jax.ShapeDtypeStruct((m, n), F32),
        scratch_shapes=[pltpu.VMEM((tm, d), BF16)],
        compiler_params=_cparams(("parallel", "arbitrary")),
        name="norm_mm",
    )(x, g.reshape(1, d), w_bf16, scale_row)


def _merge_kernel(oa_ref, ob_ref, oc_ref, g_ref, h_ref, wa_ref, wb_ref, wc_ref, wo_ref, o_ref):
    g = g_ref[...]
    ya = jnp.dot(oa_ref[...].astype(BF16), wa_ref[...], preferred_element_type=F32)
    yb = jnp.dot(ob_ref[...].astype(BF16), wb_ref[...], preferred_element_type=F32)
    yc = jnp.dot(oc_ref[...].astype(BF16), wc_ref[...], preferred_element_type=F32)
    mixed = (g[:, 0:D_MODEL] * ya + g[:, D_MODEL:2 * D_MODEL] * yb
             + g[:, 2 * D_MODEL:3 * D_MODEL] * yc)
    o_ref[...] = h_ref[...] + jnp.dot(mixed.astype(BF16), wo_ref[...], preferred_element_type=F32)


def _merge(oa, ob, oc, gates, h, wa, wb, wc, wo):
    m = h.shape[0]
    tm = 256
    row = lambda w: pl.BlockSpec((tm, w), lambda i: (i, 0))
    full = lambda a: pl.BlockSpec(a.shape, lambda i: (0, 0))
    return pl.pallas_call(
        _merge_kernel,
        grid=(m // tm,),
        in_specs=[row(A_WIDTH), row(S5_WIDTH), row(RW_WIDTH), row(3 * D_MODEL), row(D_MODEL),
                  full(wa), full(wb), full(wc), full(wo)],
        out_specs=row(D_MODEL),
        out_shape=jax.ShapeDtypeStruct((m, D_MODEL), F32),
        compiler_params=_cparams(("parallel",)),
        name="merge_out",
    )(oa, ob, oc, gates, h, wa, wb, wc, wo)


def _ffn_kernel(h_ref, g_ref, w1_ref, w2_ref, o_ref, hn_ref):
    @pl.when(pl.program_id(1) == 0)
    def _():
        h = h_ref[...]
        hn_ref[...] = _rmsnorm(h, g_ref[...]).astype(BF16)
        o_ref[...] = h

    a = jnp.dot(hn_ref[...], w1_ref[...], preferred_element_type=F32)
    a = jnp.square(jnp.maximum(a, 0.0))
    o_ref[...] += jnp.dot(a.astype(BF16), w2_ref[...], preferred_element_type=F32)


def _ffn(h, g, w1, w2, tf=1024):
    m, d = h.shape
    f = w1.shape[1]
    tm = _row_tile(m)
    return pl.pallas_call(
        _ffn_kernel,
        grid=(m // tm, f // tf),
        in_specs=[pl.BlockSpec((tm, d), lambda i, j: (i, 0)),
                  pl.BlockSpec((1, d), lambda i, j: (0, 0)),
                  pl.BlockSpec((d, tf), lambda i, j: (0, j)),
                  pl.BlockSpec((tf, d), lambda i, j: (j, 0))],
        out_specs=pl.BlockSpec((tm, d), lambda i, j: (i, 0)),
        out_shape=jax.ShapeDtypeStruct((m, d), F32),
        scratch_shapes=[pltpu.VMEM((tm, d), BF16)],
        compiler_params=_cparams(("parallel", "arbitrary")),
        name="ffn",
    )(h, g.reshape(1, d), w1, w2)


def _ple_kernel(h_ref, p_ref, g_ref, gf_ref, wg_ref, wp_ref, o_ref, y_ref):
    h = h_ref[...]
    pg = jax.nn.sigmoid(jnp.dot(_rmsnorm(h, g_ref[...]).astype(BF16), wg_ref[...],
                                preferred_element_type=F32))
    e = jnp.dot(p_ref[...].astype(BF16), wp_ref[...], preferred_element_type=F32)
    out = h + e * pg
    o_ref[...] = out
    y_ref[...] = _rmsnorm(out, gf_ref[...])


def _ple(h, ple, g, g_final, wg, wp):
    m, d = h.shape
    tm = 256
    row = lambda w: pl.BlockSpec((tm, w), lambda i: (i, 0))
    full = lambda shape: pl.BlockSpec(shape, lambda i: (0, 0))
    return pl.pallas_call(
        _ple_kernel,
        grid=(m // tm,),
        in_specs=[row(d), row(ple.shape[1]), full((1, d)), full((1, d)), full(wg.shape), full(wp.shape)],
        out_specs=[row(d), row(d)],
        out_shape=[jax.ShapeDtypeStruct((m, d), F32), jax.ShapeDtypeStruct((m, d), F32)],
        compiler_params=_cparams(("parallel",)),
        name="ple",
    )(h, ple, g.reshape(1, d), g_final.reshape(1, d), wg, wp)


def _sortable_key(s):
    b = lax.bitcast_convert_type(s, I32)
    return b ^ ((b >> 31) & 0x7FFFFFFF)


def _lane_sum(acc):
    ones = jnp.ones((LANES, LANES), BF16)
    return jnp.dot(acc.astype(F32).astype(BF16), ones, preferred_element_type=F32).astype(I32)


def _kth_largest_key(count_ge, rows, k):
    zero = jnp.zeros((rows, LANES), I32)
    c0 = count_ge(zero)
    thr = jnp.where(c0 >= k, zero, INT_MIN)
    cnt = jnp.where(c0 >= k, c0, k + 1)

    def pending(cnt):
        return jnp.max(jnp.where(cnt != k, 1, 0))

    def bit_step(it, thr, cnt):
        cand = thr + jnp.left_shift(jnp.int32(1), 30 - it)
        c = count_ge(cand)
        take = c >= k
        return jnp.where(take, cand, thr), jnp.where(take, c, cnt)

    thr, cnt = lax.fori_loop(0, SEARCH_FIXED_BITS, lambda it, s: bit_step(it, *s), (thr, cnt))

    def checked_step(state):
        it, thr, cnt, _ = state
        thr, cnt = bit_step(it, thr, cnt)
        return it + 1, thr, cnt, pending(cnt)

    state = lax.while_loop(lambda s: jnp.logical_and(s[0] < 31, s[3] > 0), checked_step,
                           (jnp.int32(SEARCH_FIXED_BITS), thr, cnt, pending(cnt)))
    return state[1]


def _t5_bucket(rel):
    n = jnp.maximum(rel, 0)
    max_exact = N_BUCKETS // 2
    nf = jnp.maximum(n, 1).astype(F32)
    large = max_exact + (jnp.log(nf / max_exact) / math.log(MAX_DISTANCE / max_exact)
                         * (N_BUCKETS - max_exact)).astype(I32)
    large = jnp.minimum(large, N_BUCKETS - 1)
    return jnp.where(n < max_exact, n, large)


def _bias_of_rel(rel_bias, rel):
    b = jnp.moveaxis(rel_bias[_t5_bucket(rel)], -1, 0).astype(F32)
    return jnp.where((rel >= 0)[None], b, NEG)


def _dsa_prompt_kernel(qb_ref, kb_ref, tid_ref, safe_ref, q_ref, k_ref, vt_ref, qi_ref, wi_ref, ki_ref,
                       d_ref, o_ref, keys_ref, thr_ref, m_ref, acc_ref, lg_ref, *, topk, ratio):
    p = pl.program_id(0)
    i = qb_ref[p]
    j = kb_ref[p]
    tq, tk = DSA_TQ, DSA_TK
    jd = i // ratio

    @pl.when(j == 0)
    def _select():
        w = wi_ref[...]
        qpos = i * tq + lax.broadcasted_iota(I32, (tq, tk), 0)
        lane = lax.broadcasted_iota(I32, (tq, tk), 1)

        def chunk(c, carry):
            off = pl.multiple_of(c * tk, tk)
            kic = ki_ref[pl.ds(off, tk), :]
            s = jnp.zeros((tq, tk), F32)
            for h in range(IDX_HEADS):
                s = s + jnp.maximum(_dot_nt(qi_ref[h], kic), 0.0) * w[:, h:h + 1]
            s = jnp.where(off + lane <= qpos, s, NEG)
            keys_ref[c] = _sortable_key(s)
            return carry

        lax.fori_loop(0, jd + 1, chunk, 0)

        def count_ge(cand):
            accs = []
            for r0 in range(0, tq, COUNT_ROWS):
                rs = slice(r0, r0 + COUNT_ROWS)
                cand_rows = cand[rs]

                def body(c, acc, rs=rs, cand_rows=cand_rows):
                    for t in range(tk // LANES):
                        ge = keys_ref[c, rs, t * LANES:(t + 1) * LANES] >= cand_rows
                        acc = acc + ge.astype(I32)
                    return acc

                accs.append(lax.fori_loop(0, jd + 1, body, jnp.zeros((COUNT_ROWS, LANES), I32)))
            return _lane_sum(jnp.concatenate(accs, axis=0))

        thr_ref[...] = _kth_largest_key(count_ge, tq, topk)
        m_ref[...] = jnp.full(m_ref.shape, NEG, F32)
        acc_ref[...] = jnp.zeros(acc_ref.shape, F32)

    thr = jnp.concatenate([thr_ref[...]] * (tk // LANES), axis=1)
    selb = jnp.where(keys_ref[j] >= thr, 0.0, NEG)
    bounded = safe_ref[0] == 1

    @pl.when(bounded)
    def _fixed_offset():
        for h in range(A_HEADS):
            pr = jnp.exp2(_dot_nt(q_ref[h], k_ref[h]) + (selb + d_ref[0, h])).astype(BF16)
            acc_ref[h] += _dot_nt(vt_ref[h], pr)

    @pl.when(jnp.logical_not(bounded))
    def _running_max():
        block_max = []
        for h in range(A_HEADS):
            lg = _dot_nt(q_ref[h], k_ref[h]) + (selb + d_ref[0, h])
            lg_ref[h] = lg
            block_max.append(jnp.max(lg, axis=1, keepdims=True))
        for h in range(A_HEADS):
            m_prev = m_ref[h]
            m_new = jnp.maximum(m_prev, block_max[h])
            pr = jnp.exp2(lg_ref[h] - m_new).astype(BF16)
            alpha = jnp.broadcast_to(jnp.exp2(m_prev - m_new), (tq, LANES)).T[0:VT_ROWS]
            acc_ref[h] = alpha * acc_ref[h] + _dot_nt(vt_ref[h], pr)
            m_ref[h] = m_new

    @pl.when(j == jd)
    def _finish():
        for h in range(A_HEADS):
            acc = acc_ref[h]
            out = acc[0:A_HEAD_DIM] / acc[A_HEAD_DIM:A_HEAD_DIM + 1]
            out = jnp.concatenate([out, jnp.zeros((LANES - A_HEAD_DIM, tq), F32)], axis=0).T
            o_ref[h] = out[:, 0:A_HEAD_DIM]


def _toeplitz_bias(rel_bias, base, rows, cols):
    n = rows + cols - 2
    rel = jnp.concatenate([base - jnp.arange(cols, dtype=I32),
                           base + rows - 1 - jnp.arange(rows - 1, dtype=I32)])
    z = _bias_of_rel(rel_bias, rel)
    m = jnp.tile(z, (1, rows))[:, :rows * n].reshape(z.shape[0], rows, n)
    return m[:, :, :cols]


def _prompt_bias_tiles(rel_bias):
    tq, tk = DSA_TQ, DSA_TK
    ratio = tk // tq
    tiles = [_toeplitz_bias(rel_bias, a * tq, tq, tk) for a in range(ratio)]
    tiles.append(_toeplitz_bias(rel_bias, tk, tq, tk))
    far = _bias_of_rel(rel_bias, jnp.full((1, 1), MAX_DISTANCE, I32))
    tiles.append(jnp.broadcast_to(far, (far.shape[0], tq, tk)))
    return jnp.stack(tiles) * LOG2E, jnp.max(jnp.abs(rel_bias)) * LOG2E


def _dsa_prompt(q, k, v, qi, wi, ki, bias):
    dtiles, bias_absmax = bias
    t = q.shape[0]
    tq, tk = DSA_TQ, DSA_TK
    ratio = tk // tq
    topk = min(TOPK_MAX, t // 4)
    n_q = t // tq
    n_kc = t // tk
    assert t % tk == 0 and t // LANES <= 256

    qb, kb, tid = [], [], []
    for i in range(n_q):
        jd, a = divmod(i, ratio)
        for j in range(jd + 1):
            qb.append(i)
            kb.append(j)
            tid.append(a if j == jd else (ratio if (j == jd - 1 and a == 0) else ratio + 1))
    qb, kb, tid = (jnp.asarray(np.asarray(x, np.int32)) for x in (qb, kb, tid))

    def heads(x):
        return x.reshape(t, -1, A_HEAD_DIM).transpose(1, 0, 2).astype(BF16)

    qh, kh = heads(q * (A_HEAD_DIM ** -0.5 * LOG2E)), heads(k)
    vt = v.reshape(t, A_HEADS, A_HEAD_DIM).transpose(1, 2, 0).astype(BF16)
    vt = jnp.concatenate([vt, jnp.ones((A_HEADS, 1, t), BF16),
                          jnp.zeros((A_HEADS, VT_ROWS - A_HEAD_DIM - 1, t), BF16)], axis=1)
    row_norm = lambda x: jnp.sqrt(jnp.max(jnp.sum(jnp.square(x.astype(F32)), axis=-1), axis=-1))
    bound = jnp.max(row_norm(qh) * row_norm(kh)) + bias_absmax
    safe = (bound <= LOGIT_BOUND).astype(I32).reshape(1)

    grid_spec = pltpu.PrefetchScalarGridSpec(
        num_scalar_prefetch=4,
        grid=(int(qb.shape[0]),),
        in_specs=[
            pl.BlockSpec((A_HEADS, tq, A_HEAD_DIM), lambda p, qb, kb, tid, sf: (0, qb[p], 0)),
            pl.BlockSpec((A_HEADS, tk, A_HEAD_DIM), lambda p, qb, kb, tid, sf: (0, kb[p], 0)),
            pl.BlockSpec((A_HEADS, VT_ROWS, tk), lambda p, qb, kb, tid, sf: (0, 0, kb[p])),
            pl.BlockSpec((IDX_HEADS, tq, IDX_DIM), lambda p, qb, kb, tid, sf: (0, qb[p], 0)),
            pl.BlockSpec((tq, IDX_HEADS), lambda p, qb, kb, tid, sf: (qb[p], 0)),
            pl.BlockSpec((t, IDX_DIM), lambda p, qb, kb, tid, sf: (0, 0)),
            pl.BlockSpec((1, A_HEADS, tq, tk), lambda p, qb, kb, tid, sf: (tid[p], 0, 0, 0)),
        ],
        out_specs=pl.BlockSpec((A_HEADS, tq, A_HEAD_DIM), lambda p, qb, kb, tid, sf: (0, qb[p], 0)),
        scratch_shapes=[pltpu.VMEM((n_kc, tq, tk), I32),
                        pltpu.VMEM((tq, LANES), I32),
                        pltpu.VMEM((A_HEADS, tq, 1), F32),
                        pltpu.VMEM((A_HEADS, VT_ROWS, tq), F32),
                        pltpu.VMEM((A_HEADS, tq, tk), F32)],
    )
    out = pl.pallas_call(
        functools.partial(_dsa_prompt_kernel, topk=topk, ratio=ratio),
        grid_spec=grid_spec,
        out_shape=jax.ShapeDtypeStruct((A_HEADS, t, A_HEAD_DIM), F32),
        compiler_params=_cparams(("arbitrary",)),
        name="dsa_prompt",
    )(qb, kb, tid, safe, qh, kh, vt, heads(qi), wi, ki.astype(BF16), dtiles)
    return out.transpose(1, 0, 2).reshape(t, A_WIDTH)


def _dsa_sample_select_kernel(pt_ref, qi_ref, wi_ref, *rest, n_pages, pps, n_new, topk):
    cki_refs = rest[:pps]
    kin_ref, keys_ref, thr_ref = rest[pps:]
    p = pl.program_id(1)
    rows = n_new
    page = kin_ref.shape[2]
    w = wi_ref[0][:, 0:1]

    def scores(kt):
        r = jnp.maximum(jnp.dot(qi_ref[0], kt.astype(BF16), preferred_element_type=F32), 0.0) * w
        return jnp.sum(r.reshape(IDX_HEADS, rows, r.shape[-1]), axis=0)

    s = scores(jnp.concatenate([r[0, 0] for r in cki_refs], axis=1))
    for r in range(pps):
        keys_ref[0, p * pps + r] = _sortable_key(s[:, r * page:(r + 1) * page])

    @pl.when(p == n_pages // pps - 1)
    def _new():
        s = scores(kin_ref[0])
        qrow = lax.broadcasted_iota(I32, s.shape, 0)
        col = lax.broadcasted_iota(I32, s.shape, 1)
        keys_ref[0, n_pages] = _sortable_key(jnp.where(col <= qrow, s, NEG))

        def count_ge(cand):
            parts = [(keys_ref[0, c] >= cand).astype(I32) for c in range(n_pages + 1)]
            while len(parts) > 1:
                parts = [a + b for a, b in zip(parts[::2], parts[1::2])] + parts[len(parts) & ~1:]
            return _lane_sum(parts[0])

        thr_ref[0] = _kth_largest_key(count_ge, rows, topk)


def _dsa_sample_attend_kernel(pt_ref, q_ref, keys_ref, thr_ref, *rest, n_pages, pps):
    ck_refs, cv_refs = rest[:pps], rest[pps:2 * pps]
    (kn_ref, vn_ref, dfar_ref, dlast_ref, dnew_ref, o_ref, m_ref, l_ref, acc_ref) = rest[2 * pps:]
    p = pl.program_id(1)
    last = p == n_pages // pps - 1
    rows = keys_ref.shape[2]
    hd = A_HEADS * A_HEAD_DIM

    @pl.when(p == 0)
    def _init():
        m_ref[...] = jnp.full(m_ref.shape, NEG, F32)
        l_ref[...] = jnp.zeros(l_ref.shape, F32)
        acc_ref[...] = jnp.zeros(acc_ref.shape, F32)

    thr = thr_ref[0]

    def attend(kt, vt, key_pages, dtile):
        selb = jnp.concatenate([jnp.where(kp >= thr, 0.0, NEG) for kp in key_pages], axis=1)
        selb = jnp.concatenate([selb] * A_HEADS, axis=0)
        lg = jnp.dot(q_ref[0], kt, preferred_element_type=F32) + (selb + dtile)
        m_prev = m_ref[...]
        m_new = jnp.maximum(m_prev, jnp.max(lg, axis=1, keepdims=True))
        alpha = jnp.exp2(m_prev - m_new)
        pr = jnp.exp2(lg - m_new)
        l_ref[...] = alpha * l_ref[...] + jnp.sum(pr, axis=1, keepdims=True)
        acc_ref[...] = alpha * acc_ref[...] + _dot_nt(pr.astype(BF16), vt)
        m_ref[...] = m_new

    kt = jnp.concatenate([r[0, 0].reshape(hd, -1).astype(BF16) for r in ck_refs], axis=1)
    vt = jnp.concatenate([r[0, 0].reshape(hd, -1).astype(BF16) for r in cv_refs], axis=1)
    attend(kt, vt, [keys_ref[0, p * pps + r] for r in range(pps)],
           jnp.where(last, dlast_ref[...], dfar_ref[...]))

    @pl.when(last)
    def _new():
        attend(kn_ref[0].astype(BF16), vn_ref[0].astype(BF16), [keys_ref[0, n_pages]], dnew_ref[...])
        acc = acc_ref[...]
        l = l_ref[...]
        for h in range(A_HEADS):
            rs = slice(h * rows, (h + 1) * rows)
            o_ref[0, rs, :] = acc[rs, h * A_HEAD_DIM:(h + 1) * A_HEAD_DIM] / l[rs]


def _sample_bias_tiles(rel_bias, n_new, page):
    pps = SAMPLE_PAGES_PER_STEP
    qq = jnp.arange(n_new, dtype=I32)[:, None]
    pos = jnp.arange(page, dtype=I32)[None, :]
    tile = lambda rel: _bias_of_rel(rel_bias, rel).reshape(A_HEADS * n_new, page) * LOG2E
    far = tile(jnp.full((n_new, page), MAX_DISTANCE, I32))
    last = tile(page + qq - pos)
    new = tile(qq - pos)
    return jnp.tile(far, (1, pps)), jnp.concatenate([far] * (pps - 1) + [last], axis=1), new


def _dsa_sample(layer, q, k_new, v_new, qi, wi, ki_new, cache_k, cache_v, cache_idx_k,
                page_table, dtiles):
    bd, n_pages = page_table.shape
    page = cache_k.shape[2]
    n_new = q.shape[0] // bd
    past = n_pages * page
    topk = min(TOPK_MAX, (past + n_new) // 4)
    hrows = A_HEADS * n_new
    pcols = page * A_HEADS
    pps = SAMPLE_PAGES_PER_STEP
    n_steps = n_pages // pps
    assert n_pages % pps == 0 and n_pages + 1 <= 256
    pt = page_table.reshape(-1).astype(I32)

    def head_rows(x):
        return x.reshape(bd, n_new, -1, A_HEAD_DIM).transpose(0, 2, 1, 3).reshape(bd, hrows, A_HEAD_DIM)

    qi2 = head_rows(qi).astype(BF16)
    wi2 = jnp.broadcast_to(wi.reshape(bd, n_new, IDX_HEADS).transpose(0, 2, 1).reshape(bd, hrows, 1),
                           (bd, hrows, LANES))

    def lanes_last(x):
        x = x.reshape(bd, n_new, -1).transpose(0, 2, 1)
        return jnp.pad(x, ((0, 0), (0, 0), (0, page - n_new)))

    cki_t = cache_idx_k.transpose(0, 1, 3, 2)
    ck_t = cache_k.transpose(0, 1, 3, 4, 2)
    cv_t = cache_v.transpose(0, 1, 3, 4, 2)

    def page_spec(block, r, pps=pps):
        zeros = (0,) * (len(block) - 2)
        return pl.BlockSpec(block, lambda b, p, pt: (layer, pt[b * n_pages + p * pps + r]) + zeros)

    per_batch = lambda block: pl.BlockSpec(block, lambda b, p, pt: (b,) + (0,) * (len(block) - 1))
    const = lambda block: pl.BlockSpec(block, lambda b, p, pt: (0,) * len(block))

    spp = 2 * pps if n_pages % (2 * pps) == 0 else pps
    keys, thr = pl.pallas_call(
        functools.partial(_dsa_sample_select_kernel, n_pages=n_pages, pps=spp, n_new=n_new, topk=topk),
        grid_spec=pltpu.PrefetchScalarGridSpec(
            num_scalar_prefetch=1,
            grid=(bd, n_pages // spp),
            in_specs=([per_batch((1, hrows, IDX_DIM)), per_batch((1, hrows, LANES))]
                      + [page_spec((1, 1, IDX_DIM, page), r, spp) for r in range(spp)]
                      + [per_batch((1, IDX_DIM, page))]),
            out_specs=[per_batch((1, n_pages + 1, n_new, page)), per_batch((1, n_new, LANES))],
        ),
        out_shape=[jax.ShapeDtypeStruct((bd, n_pages + 1, n_new, page), I32),
                   jax.ShapeDtypeStruct((bd, n_new, LANES), I32)],
        compiler_params=_cparams(("parallel", "arbitrary")),
        name="dsa_sample_select",
    )(pt, qi2, wi2, *([cki_t] * spp), lanes_last(ki_new))

    dfar, dlast, dnew = dtiles
    q4 = (q * (A_HEAD_DIM ** -0.5 * LOG2E)).reshape(bd, n_new, A_HEADS, A_HEAD_DIM)
    qbd = jnp.einsum("bqhd,hg->bhqgd", q4, jnp.eye(A_HEADS, dtype=F32)).reshape(bd, hrows, A_WIDTH)

    cache_block = (1, 1, A_HEADS, A_HEAD_DIM, page)
    out = pl.pallas_call(
        functools.partial(_dsa_sample_attend_kernel, n_pages=n_pages, pps=pps),
        grid_spec=pltpu.PrefetchScalarGridSpec(
            num_scalar_prefetch=1,
            grid=(bd, n_steps),
            in_specs=([per_batch((1, hrows, A_WIDTH)), per_batch((1, n_pages + 1, n_new, page)),
                       per_batch((1, n_new, LANES))]
                      + [page_spec(cache_block, r) for r in range(pps)]
                      + [page_spec(cache_block, r) for r in range(pps)]
                      + [per_batch((1, A_WIDTH, page)), per_batch((1, A_WIDTH, page)),
                         const((hrows, pps * page)), const((hrows, pps * page)), const((hrows, page))]),
            out_specs=per_batch((1, hrows, A_HEAD_DIM)),
            scratch_shapes=[pltpu.VMEM((hrows, 1), F32), pltpu.VMEM((hrows, 1), F32),
                            pltpu.VMEM((hrows, A_WIDTH), F32)],
        ),
        out_shape=jax.ShapeDtypeStruct((bd, hrows, A_HEAD_DIM), F32),
        compiler_params=_cparams(("parallel", "arbitrary")),
        name="dsa_sample_attend",
    )(pt, qbd.astype(BF16), keys, thr, *([ck_t] * pps), *([cv_t] * pps),
      lanes_last(k_new), lanes_last(v_new), dfar, dlast, dnew)
    return out.reshape(bd, A_HEADS, n_new, A_HEAD_DIM).transpose(0, 2, 1, 3).reshape(bd * n_new, A_WIDTH)


def _s5_in_kernel(u_ref, bb_ref, o_ref):
    o_ref[...] = _dot_exact(u_ref[...], bb_ref[...])


def _s5_scan_kernel(bu_ref, s0_ref, ab_ref, o_ref, fin_ref, st_ref, *, tc):
    @pl.when(pl.program_id(1) == 0)
    def _():
        st_ref[...] = s0_ref[0]

    ar, ai = ab_ref[0], ab_ref[1]

    def step(t, carry):
        sr, si = carry
        nr = ar * sr - ai * si + bu_ref[0, t, 0]
        ni = ar * si + ai * sr + bu_ref[0, t, 1]
        o_ref[0, t, 0] = nr
        o_ref[0, t, 1] = ni
        return nr, ni

    sr, si = lax.fori_loop(0, tc, step, (st_ref[0], st_ref[1]), unroll=8)
    st_ref[0] = sr
    st_ref[1] = si
    fin_ref[0, 0] = sr
    fin_ref[0, 1] = si


def _s5_out_kernel(s_ref, u_ref, c_ref, d_ref, wg_ref, o_ref):
    u = u_ref[...]
    y = _dot_exact(s_ref[...], c_ref[...]) + u * d_ref[...]
    hdn = jax.nn.gelu(y)
    gate = jax.nn.sigmoid(jnp.dot(hdn.astype(BF16), wg_ref[...], preferred_element_type=F32))
    o_ref[...] = hdn * gate


def _s5(u, s0_re, s0_im, lam_re, lam_im, log_step, b_re, b_im, c_re, c_im, d, w_glu, batch):
    m = u.shape[0]
    t = m // batch
    g, p = S5_GROUPS, S5_STATE
    n_state = g * p
    lam_re = jnp.minimum(lam_re.astype(F32), -1e-4)
    lam_im = lam_im.astype(F32)
    dt = jnp.exp(log_step.astype(F32))[:, None]
    mag = jnp.exp(lam_re * dt)
    ab_re, ab_im = mag * jnp.cos(lam_im * dt), mag * jnp.sin(lam_im * dt)
    den = lam_re * lam_re + lam_im * lam_im
    nr, ni = ab_re - 1.0, ab_im
    f_re = (nr * lam_re + ni * lam_im) / den
    f_im = (ni * lam_re - nr * lam_im) / den
    bb_re = f_re[..., None] * b_re - f_im[..., None] * b_im
    bb_im = f_re[..., None] * b_im + f_im[..., None] * b_re
    eye = jnp.eye(g, dtype=F32)
    blk_in = lambda x: jnp.einsum("gph,gk->ghkp", x, eye).reshape(S5_WIDTH, n_state)
    blk_out = lambda x: jnp.einsum("ghp,gk->kpgh", x, eye).reshape(n_state, S5_WIDTH)
    bb = jnp.concatenate([blk_in(bb_re), blk_in(bb_im)], axis=1)
    cc = jnp.concatenate([blk_out(c_re.astype(F32)), -blk_out(c_im.astype(F32))], axis=0)
    ab = jnp.stack([ab_re.reshape(SUBLANES, LANES), ab_im.reshape(SUBLANES, LANES)])
    s0 = jnp.stack([s0_re.reshape(batch, SUBLANES, LANES), s0_im.reshape(batch, SUBLANES, LANES)], axis=1)

    tm = 256
    bu = pl.pallas_call(
        _s5_in_kernel,
        grid=(m // tm,),
        in_specs=[pl.BlockSpec((tm, S5_WIDTH), lambda i: (i, 0)),
                  pl.BlockSpec(bb.shape, lambda i: (0, 0))],
        out_specs=pl.BlockSpec((tm, 2 * n_state), lambda i: (i, 0)),
        out_shape=jax.ShapeDtypeStruct((m, 2 * n_state), F32),
        compiler_params=_cparams(("parallel",)),
        name="s5_in",
    )(u, bb)

    tc = min(t, 256)
    tile = (1, tc, 2, SUBLANES, LANES)
    states, fin = pl.pallas_call(
        functools.partial(_s5_scan_kernel, tc=tc),
        grid=(batch, t // tc),
        in_specs=[pl.BlockSpec(tile, lambda b, c: (b, c, 0, 0, 0)),
                  pl.BlockSpec((1, 2, SUBLANES, LANES), lambda b, c: (b, 0, 0, 0)),
                  pl.BlockSpec((2, SUBLANES, LANES), lambda b, c: (0, 0, 0))],
        out_specs=[pl.BlockSpec(tile, lambda b, c: (b, c, 0, 0, 0)),
                   pl.BlockSpec((1, 2, SUBLANES, LANES), lambda b, c: (b, 0, 0, 0))],
        out_shape=[jax.ShapeDtypeStruct((batch, t, 2, SUBLANES, LANES), F32),
                   jax.ShapeDtypeStruct((batch, 2, SUBLANES, LANES), F32)],
        scratch_shapes=[pltpu.VMEM((2, SUBLANES, LANES), F32)],
        compiler_params=_cparams(("parallel", "arbitrary")),
        name="s5_scan",
    )(bu.reshape(batch, t, 2, SUBLANES, LANES), s0, ab)

    out = pl.pallas_call(
        _s5_out_kernel,
        grid=(m // tm,),
        in_specs=[pl.BlockSpec((tm, 2 * n_state), lambda i: (i, 0)),
                  pl.BlockSpec((tm, S5_WIDTH), lambda i: (i, 0)),
                  pl.BlockSpec(cc.shape, lambda i: (0, 0)),
                  pl.BlockSpec((1, S5_WIDTH), lambda i: (0, 0)),
                  pl.BlockSpec((S5_WIDTH, S5_WIDTH), lambda i: (0, 0))],
        out_specs=pl.BlockSpec((tm, S5_WIDTH), lambda i: (i, 0)),
        out_shape=jax.ShapeDtypeStruct((m, S5_WIDTH), F32),
        compiler_params=_cparams(("parallel",)),
        name="s5_out",
    )(states.reshape(m, 2 * n_state), u, cc, d.reshape(1, S5_WIDTH).astype(F32), w_glu.astype(BF16))
    return out, fin[:, 0].reshape(batch, g, p), fin[:, 1].reshape(batch, g, p)


_RW_OPS = ("r", "k", "v", "d", "kk", "ka")


def _head_sum_matrix():
    hid = np.arange(RW_WIDTH) // RW_HEAD_DIM
    return jnp.asarray((hid[:, None] == hid[None, :]).astype(np.float32))


def _rwkv_pre_kernel(*refs, with_vres):
    if with_vres:
        (pc_ref, prev_ref, vf_ref, mu_ref, w0_ref, w2_ref, a0_ref, a2_ref, g2_ref, kk_ref, ka_ref,
         rk_ref, j_ref, v0_ref, v1_ref, v2_ref, ops_ref, gb_ref) = refs
    else:
        (pc_ref, prev_ref, mu_ref, w0_ref, w2_ref, a0_ref, a2_ref, g2_ref, kk_ref, ka_ref,
         rk_ref, j_ref, ops_ref, gb_ref) = refs
    pc = pc_ref[...]
    xm = pc + (prev_ref[...] - pc) * mu_ref[...]
    w_ = RW_WIDTH
    r, k, v = xm[:, 0:w_], xm[:, w_:2 * w_], xm[:, 2 * w_:3 * w_]
    o = 3 * w_
    lora = xm[:, o:o + RW_DECAY_LORA + RW_AAA_LORA]
    gl = xm[:, o + RW_DECAY_LORA + RW_AAA_LORA:]
    mm = lambda a, b_ref: jnp.dot(a.astype(BF16), b_ref[...].astype(BF16), preferred_element_type=F32)
    z = -(w0_ref[...] + mm(jnp.tanh(lora), w2_ref))
    softplus = jnp.maximum(z, 0.0) + jnp.log(1.0 + jnp.exp(-jnp.abs(z)))
    w = -softplus - 0.5
    decay = jnp.exp(-jnp.exp(w))
    a = jax.nn.sigmoid(a0_ref[...] + mm(lora, a2_ref))
    g = mm(jax.nn.sigmoid(gl), g2_ref)
    if with_vres:
        v = v + (vf_ref[...] - v) * jax.nn.sigmoid(v0_ref[...] + mm(mm(v, v1_ref), v2_ref))
    jm = j_ref[...]
    kk = k * kk_ref[...]
    kk = kk * lax.rsqrt(jnp.maximum(_dot_exact(kk * kk, jm), 1e-24))
    k = k * (1.0 + (a - 1.0) * ka_ref[...])
    bonus = _dot_exact(r * k * rk_ref[...], jm) * v
    for idx, val in enumerate((r, k, v, decay, kk, kk * a)):
        ops_ref[:, idx * w_:(idx + 1) * w_] = val
    gb_ref[:, 0:w_] = g
    gb_ref[:, w_:2 * w_] = bonus


def _rwkv_scan_kernel(ops_ref, s0_ref, mask_ref, j_ref, y_ref, fin_ref, st_ref, zh_ref, zl_ref, e_ref,
                      *, tc, sub):
    @pl.when(pl.program_id(1) == 0)
    def _():
        st_ref[...] = s0_ref[0]

    n, w_ = RW_HEAD_DIM, RW_WIDTH
    col = {name: idx * w_ for idx, name in enumerate(_RW_OPS)}
    expand = ("d", "kk", "ka", "k", "r")
    mask = mask_ref[...] > 0.5
    jm = j_ref[...]

    def sub_chunk(sc, state):
        t0 = pl.multiple_of(sc * sub, sub)
        for oi, name in enumerate(expand):
            two_terms = name == "d"
            for t in range(sub):
                row = ops_ref[0, pl.ds(t0 + t, 1), col[name]:col[name] + w_]
                hi = row.astype(BF16).astype(F32)
                zh_ref[t * n:(t + 1) * n, :] = jnp.where(mask, jnp.broadcast_to(hi, (n, w_)), 0.0).astype(BF16)
                if two_terms:
                    lo = row - hi
                    zl_ref[t * n:(t + 1) * n, :] = jnp.where(mask, jnp.broadcast_to(lo, (n, w_)), 0.0).astype(BF16)
            e = jnp.dot(zh_ref[...], jm, preferred_element_type=F32)
            if two_terms:
                e = e + jnp.dot(zl_ref[...], jm, preferred_element_type=F32)
            e_ref[oi] = e
        for t in range(sub):
            rows = slice(t * n, (t + 1) * n)
            vrow = ops_ref[0, pl.ds(t0 + t, 1), col["v"]:col["v"] + w_]
            sa = -jnp.sum(state * e_ref[1, rows, :], axis=0, keepdims=True)
            state = e_ref[0, rows, :] * state + e_ref[2, rows, :] * sa + e_ref[3, rows, :] * vrow
            y_ref[0, pl.ds(t0 + t, 1), :] = jnp.sum(state * e_ref[4, rows, :], axis=0, keepdims=True)
        return state

    state = lax.fori_loop(0, tc // sub, sub_chunk, st_ref[...])
    st_ref[...] = state
    fin_ref[0] = state


def _rwkv_post_kernel(y_ref, gb_ref, lw_ref, lb_ref, j_ref, o_ref):
    y = y_ref[...]
    jm = j_ref[...]
    inv = 1.0 / RW_HEAD_DIM
    mean = _dot_exact(y, jm) * inv
    yc = y - mean
    var = _dot_exact(yc * yc, jm) * inv
    yn = yc * lax.rsqrt(var + RW_GN_EPS) * lw_ref[...] + lb_ref[...]
    gb = gb_ref[...]
    o_ref[...] = (yn + gb[:, RW_WIDTH:]) * gb[:, 0:RW_WIDTH]


def _rwkv(pc, shift_prev, wkv0, v_first, vres, mu, w0, w2, a0, a2, g2, k_k, k_a, r_k, ln_w, ln_b, batch):
    m = pc.shape[0]
    t = m // batch
    w_ = RW_WIDTH
    pc3 = pc.reshape(batch, t, RW_PROJ)
    prev = jnp.concatenate([shift_prev.astype(F32)[:, None], pc3[:, :-1]], axis=1).reshape(m, RW_PROJ)
    jm = _head_sum_matrix()
    row = lambda x: x.reshape(1, -1).astype(F32)
    tm = 256
    rspec = lambda width: pl.BlockSpec((tm, width), lambda i: (i, 0))
    fspec = lambda a: pl.BlockSpec(a.shape, lambda i: (0,) * a.ndim)
    w2p = jnp.pad(w2, ((0, RW_AAA_LORA), (0, 0)))
    a2p = jnp.pad(a2, ((RW_DECAY_LORA, 0), (0, 0)))
    params = [row(mu), row(w0), w2p, row(a0), a2p, g2, row(k_k), row(k_a), row(r_k), jm]
    ins = [pc, prev]
    specs = [rspec(RW_PROJ), rspec(RW_PROJ)]
    if vres is not None:
        ins.append(v_first)
        specs.append(rspec(w_))
        params += [row(vres[0]), vres[1], vres[2]]
    ops, gb = pl.pallas_call(
        functools.partial(_rwkv_pre_kernel, with_vres=vres is not None),
        grid=(m // tm,),
        in_specs=specs + [fspec(a) for a in params],
        out_specs=[rspec(len(_RW_OPS) * w_), rspec(2 * w_)],
        out_shape=[jax.ShapeDtypeStruct((m, len(_RW_OPS) * w_), F32),
                   jax.ShapeDtypeStruct((m, 2 * w_), F32)],
        compiler_params=_cparams(("parallel",)),
        name="rwkv_pre",
    )(*ins, *params)
    if vres is None:
        v_first = ops[:, 2 * w_:3 * w_]

    n = RW_HEAD_DIM
    sub = 8
    tc = min(t, 64)
    jidx = np.arange(w_) % n
    mask = jnp.asarray((np.arange(n)[:, None] == jidx[None, :]).astype(np.float32))
    p0 = wkv0.astype(F32).transpose(0, 3, 1, 2).reshape(batch, n, w_)
    y, fin = pl.pallas_call(
        functools.partial(_rwkv_scan_kernel, tc=tc, sub=sub),
        grid=(batch, t // tc),
        in_specs=[pl.BlockSpec((1, tc, len(_RW_OPS) * w_), lambda b, c: (b, c, 0)),
                  pl.BlockSpec((1, n, w_), lambda b, c: (b, 0, 0)),
                  pl.BlockSpec((n, w_), lambda b, c: (0, 0)),
                  pl.BlockSpec((w_, w_), lambda b, c: (0, 0))],
        out_specs=[pl.BlockSpec((1, tc, w_), lambda b, c: (b, c, 0)),
                   pl.BlockSpec((1, n, w_), lambda b, c: (b, 0, 0))],
        out_shape=[jax.ShapeDtypeStruct((batch, t, w_), F32),
                   jax.ShapeDtypeStruct((batch, n, w_), F32)],
        scratch_shapes=[pltpu.VMEM((n, w_), F32),
                        pltpu.VMEM((sub * n, w_), BF16),
                        pltpu.VMEM((sub * n, w_), BF16),
                        pltpu.VMEM((5, sub * n, w_), F32)],
        compiler_params=_cparams(("parallel", "arbitrary")),
        name="rwkv_scan",
    )(ops.reshape(batch, t, len(_RW_OPS) * w_), p0, mask, jm.astype(BF16))
    wkv = fin.reshape(batch, n, RW_HEADS, n).transpose(0, 2, 3, 1)

    out = pl.pallas_call(
        _rwkv_post_kernel,
        grid=(m // tm,),
        in_specs=[rspec(w_), rspec(2 * w_), fspec(row(ln_w)), fspec(row(ln_b)), fspec(jm)],
        out_specs=rspec(w_),
        out_shape=jax.ShapeDtypeStruct((m, w_), F32),
        compiler_params=_cparams(("parallel",)),
        name="rwkv_post",
    )(y.reshape(m, w_), gb, row(ln_w), row(ln_b), jm)
    return out, wkv, v_first


def _relayout_w_in(w_in):
    cols = []
    for name, width in _PIECES:
        src, _, _ = _LAYOUT[name]
        piece = w_in[:, src:src + width]
        pad = -(-width // LANES) * LANES - width
        cols.append(jnp.pad(piece, ((0, 0), (0, pad))) if pad else piece)
    return jnp.concatenate(cols, axis=1).astype(BF16)


def _proj_scale_row():
    s = np.ones((1, PROJ_COLS), np.float32)
    _, d0, w0 = _LAYOUT["qi"]
    s[0, d0:d0 + w0] = IDX_DIM ** -0.5
    _, d1, w1 = _LAYOUT["wi"]
    s[0, d1:d1 + w1] = IDX_HEADS ** -0.5
    return jnp.asarray(s)


def _trunk(x, ple, attend, s5_re0, s5_im0, wkv0, shift0, W):
    batch, t, _ = x.shape
    m = batch * t
    depth = W["w_in"].shape[0]
    h = x.reshape(m, D_MODEL)
    v_first = None
    ks, vs, kis, sres, sims, wkvs, shifts = [], [], [], [], [], [], []
    scale_row = _proj_scale_row()
    ones_row = jnp.ones((1, 3 * D_MODEL), F32)
    y = None

    def piece(proj, name):
        _, dst, width = _LAYOUT[name]
        return proj[:, dst:dst + width]

    for i in range(depth):
        proj = _norm_mm(h, W["norm_mix"][i], _relayout_w_in(W["w_in"][i]), scale_row, tn=PROJ_COLS // 4)
        gates = _norm_mm(h, W["norm_mix"][i], W["w_gate"][i].astype(BF16), ones_row, act="sigmoid",
                         tn=D_MODEL)
        q, k, v, qi, wi, ki, u, pc = (piece(proj, n) for n, _ in _PIECES)
        o_a = attend(i, q, k, v, qi, wi, ki)
        o_b, s_re, s_im = _s5(u, s5_re0[i], s5_im0[i], W["ssm_lambda_re"][i], W["ssm_lambda_im"][i],
                              W["ssm_log_step"][i], W["ssm_b_re"][i], W["ssm_b_im"][i],
                              W["ssm_c_re"][i], W["ssm_c_im"][i], W["ssm_d"][i], W["ssm_w_glu"][i], batch)
        vres = None if i == 0 else (W["rw_v0"][i - 1], W["rw_v1"][i - 1], W["rw_v2"][i - 1])
        o_c, wkv, v_first = _rwkv(pc, shift0[i], wkv0[i], v_first, vres, W["rw_mu"][i], W["rw_w0"][i],
                                  W["rw_w2"][i], W["rw_a0"][i], W["rw_a2"][i], W["rw_g2"][i],
                                  W["rw_k_k"][i], W["rw_k_a"][i], W["rw_r_k"][i], W["rw_ln_w"][i],
                                  W["rw_ln_b"][i], batch)
        h = _merge(o_a, o_b, o_c, gates, h, W["w_up_a"][i].astype(BF16), W["w_up_b"][i].astype(BF16),
                   W["w_up_c"][i].astype(BF16), W["w_out"][i].astype(BF16))
        h = _ffn(h, W["norm_ffn"][i], W["w_ff1"][i].astype(BF16), W["w_ff2"][i].astype(BF16))
        h, y = _ple(h, ple[i].reshape(m, -1), W["norm_ple"][i], W["norm_final"],
                    W["w_ple_gate"][i].astype(BF16), W["w_ple"][i].astype(BF16))
        ks.append(k.reshape(batch, t, A_HEADS, A_HEAD_DIM))
        vs.append(v.reshape(batch, t, A_HEADS, A_HEAD_DIM))
        kis.append(ki.reshape(batch, t, IDX_DIM))
        sres.append(s_re)
        sims.append(s_im)
        wkvs.append(wkv)
        shifts.append(pc.reshape(batch, t, RW_PROJ)[:, -1])
    return (y.reshape(batch, t, D_MODEL), jnp.stack(ks), jnp.stack(vs), jnp.stack(kis), jnp.stack(sres),
            jnp.stack(sims), jnp.stack(wkvs), jnp.stack(shifts))


def kernel(x_prompt, x_sample, cache_k, cache_v, cache_idx_k, state_ssm_re, state_ssm_im, state_wkv,
           state_shift, page_table, p_prompt, p_sample, rel_bias, norm_mix, norm_ffn, norm_ple, norm_final,
           w_in, w_up_a, w_up_b, w_up_c, w_gate, w_out, ssm_lambda_re, ssm_lambda_im, ssm_log_step,
           ssm_b_re, ssm_b_im, ssm_c_re, ssm_c_im, ssm_d, ssm_w_glu, rw_mu, rw_w0, rw_w2, rw_a0, rw_a2,
           rw_g2, rw_k_k, rw_k_a, rw_r_k, rw_ln_w, rw_ln_b, rw_v0, rw_v1, rw_v2, w_ff1, w_ff2,
           w_ple, w_ple_gate):
    W = dict(norm_mix=norm_mix, norm_ffn=norm_ffn, norm_ple=norm_ple, norm_final=norm_final, w_in=w_in,
             w_up_a=w_up_a, w_up_b=w_up_b, w_up_c=w_up_c, w_gate=w_gate, w_out=w_out,
             ssm_lambda_re=ssm_lambda_re, ssm_lambda_im=ssm_lambda_im, ssm_log_step=ssm_log_step,
             ssm_b_re=ssm_b_re, ssm_b_im=ssm_b_im, ssm_c_re=ssm_c_re, ssm_c_im=ssm_c_im, ssm_d=ssm_d,
             ssm_w_glu=ssm_w_glu, rw_mu=rw_mu, rw_w0=rw_w0, rw_w2=rw_w2, rw_a0=rw_a0, rw_a2=rw_a2,
             rw_g2=rw_g2, rw_k_k=rw_k_k, rw_k_a=rw_k_a, rw_r_k=rw_r_k, rw_ln_w=rw_ln_w, rw_ln_b=rw_ln_b,
             rw_v0=rw_v0, rw_v1=rw_v1, rw_v2=rw_v2, w_ff1=w_ff1, w_ff2=w_ff2, w_ple=w_ple,
             w_ple_gate=w_ple_gate)
    depth = w_in.shape[0]
    bp = x_prompt.shape[0]

    prompt_tiles = _prompt_bias_tiles(rel_bias)
    sample_tiles = _sample_bias_tiles(rel_bias, x_sample.shape[1], cache_k.shape[2])

    def attend_prompt(i, q, k, v, qi, wi, ki):
        return _dsa_prompt(q, k, v, qi, wi, ki, prompt_tiles)

    def attend_sample(i, q, k, v, qi, wi, ki):
        return _dsa_sample(i, q, k, v, qi, wi, ki, cache_k, cache_v, cache_idx_k, page_table, sample_tiles)

    zs = jnp.zeros((depth, bp, S5_GROUPS, S5_STATE), F32)
    zw = jnp.zeros((depth, bp, RW_HEADS, RW_HEAD_DIM, RW_HEAD_DIM), F32)
    zsh = jnp.zeros((depth, bp, RW_PROJ), x_prompt.dtype)
    (y_prompt, k_prompt, v_prompt, idx_k_prompt, ssm_re_prompt, ssm_im_prompt, wkv_prompt,
     shift_prompt) = _trunk(x_prompt, p_prompt, attend_prompt, zs, zs, zw, zsh, W)
    (y_sample, k_sample, v_sample, idx_k_sample, ssm_re_sample, ssm_im_sample, wkv_sample,
     shift_sample) = _trunk(x_sample, p_sample, attend_sample, state_ssm_re, state_ssm_im, state_wkv,
                            state_shift, W)
    return (y_prompt, y_sample, k_prompt, v_prompt, idx_k_prompt, k_sample, v_sample, idx_k_sample,
            ssm_re_prompt, ssm_im_prompt, ssm_re_sample, ssm_im_sample, wkv_prompt, wkv_sample,
            shift_prompt, shift_sample)
```

```python
import functools
import math

import numpy as np
import jax
import jax.numpy as jnp
from jax import lax
from jax.experimental import pallas as pl
from jax.experimental.pallas import tpu as pltpu

F32 = jnp.float32
BF16 = jnp.bfloat16
I32 = jnp.int32

D_MODEL = 1024
A_HEADS = 8
A_HEAD_DIM = 64
A_WIDTH = A_HEADS * A_HEAD_DIM
IDX_HEADS = 8
IDX_DIM = 64
TOPK_MAX = 256
N_BUCKETS = 32
MAX_DISTANCE = 128
S5_GROUP = 16
S5_GROUPS = 16
S5_WIDTH = 256
S5_STATE = 64
RW_HEADS = 4
RW_HEAD_DIM = 64
RW_WIDTH = 256
RW_DECAY_LORA = 64
RW_AAA_LORA = 64
RW_GATE_LORA = 128
RW_PROJ = 1024
RW_GN_EPS = 64e-5
D_FF = 4 * D_MODEL
EPS = 1e-6
NEG = -1e30
INT_MIN = -(2 ** 31)
LOG2E = math.log2(math.e)
LOGIT_BOUND = 60.0

LANES = 128
SUBLANES = 8
VMEM_LIMIT = 56 * 1024 * 1024

_PIECES = (("q", A_WIDTH), ("k", A_WIDTH), ("v", A_WIDTH), ("qi", IDX_HEADS * IDX_DIM),
           ("wi", IDX_HEADS), ("ki", IDX_DIM), ("u", S5_WIDTH), ("pc", RW_PROJ))


def _piece_layout():
    src, dst, out = 0, 0, {}
    for name, width in _PIECES:
        padded = -(-width // LANES) * LANES
        out[name] = (src, dst, width)
        src += width
        dst += padded
    return out, dst


_LAYOUT, PROJ_COLS = _piece_layout()

DSA_TQ = 256
DSA_TK = 512
SAMPLE_PAGES_PER_STEP = 8
VT_ROWS = 80
COUNT_ROWS = 128
SEARCH_FIXED_BITS = 18


def _cparams(sem):
    return pltpu.CompilerParams(dimension_semantics=sem, vmem_limit_bytes=VMEM_LIMIT)


def _rmsnorm(x, g):
    return x * lax.rsqrt(jnp.mean(x * x, axis=-1, keepdims=True) + EPS) * g


def _dot_exact(a, b):
    return jnp.dot(a, b, precision=lax.Precision.HIGHEST, preferred_element_type=F32)


def _dot_nt(a, b):
    return lax.dot_general(a, b, (((1,), (1,)), ((), ())), preferred_element_type=F32)


def _row_tile(m):
    return 512 if m % 512 == 0 else 256


def _norm_mm_kernel(x_ref, g_ref, w_ref, s_ref, o_ref, xn_ref, *, act):
    @pl.when(pl.program_id(1) == 0)
    def _():
        xn_ref[...] = _rmsnorm(x_ref[...], g_ref[...]).astype(BF16)

    y = jnp.dot(xn_ref[...], w_ref[...], preferred_element_type=F32)
    if act == "sigmoid":
        y = jax.nn.sigmoid(y)
    o_ref[...] = y * s_ref[...]


def _norm_mm(x, g, w_bf16, scale_row, act=None, tn=512):
    m, d = x.shape
    n = w_bf16.shape[1]
    tm = _row_tile(m)
    return pl.pallas_call(
        functools.partial(_norm_mm_kernel, act=act),
        grid=(m // tm, n // tn),
        in_specs=[pl.BlockSpec((tm, d), lambda i, j: (i, 0)),
                  pl.BlockSpec((1, d), lambda i, j: (0, 0)),
                  pl.BlockSpec((d, tn), lambda i, j: (0, j)),
                  pl.BlockSpec((1, tn), lambda i, j: (0, j))],
        out_specs=pl.BlockSpec((tm, tn), lambda i, j: (i, j)),
        out_shape=jax.ShapeDtypeStruct((m, n), F32),
        scratch_shapes=[pltpu.VMEM((tm, d), BF16)],
        compiler_params=_cparams(("parallel", "arbitrary")),
        name="norm_mm",
    )(x, g.reshape(1, d), w_bf16, scale_row)


def _merge_kernel(oa_ref, ob_ref, oc_ref, g_ref, h_ref, wa_ref, wb_ref, wc_ref, wo_ref, o_ref):
    g = g_ref[...]
    ya = jnp.dot(oa_ref[...].astype(BF16), wa_ref[...], preferred_element_type=F32)
    yb = jnp.dot(ob_ref[...].astype(BF16), wb_ref[...], preferred_element_type=F32)
    yc = jnp.dot(oc_ref[...].astype(BF16), wc_ref[...], preferred_element_type=F32)
    mixed = (g[:, 0:D_MODEL] * ya + g[:, D_MODEL:2 * D_MODEL] * yb
             + g[:, 2 * D_MODEL:3 * D_MODEL] * yc)
    o_ref[...] = h_ref[...] + jnp.dot(mixed.astype(BF16), wo_ref[...], preferred_element_type=F32)


def _merge(oa, ob, oc, gates, h, wa, wb, wc, wo):
    m = h.shape[0]
    tm = 256
    row = lambda w: pl.BlockSpec((tm, w), lambda i: (i, 0))
    full = lambda a: pl.BlockSpec(a.shape, lambda i: (0, 0))
    return pl.pallas_call(
        _merge_kernel,
        grid=(m // tm,),
        in_specs=[row(A_WIDTH), row(S5_WIDTH), row(RW_WIDTH), row(3 * D_MODEL), row(D_MODEL),
                  full(wa), full(wb), full(wc), full(wo)],
        out_specs=row(D_MODEL),
        out_shape=jax.ShapeDtypeStruct((m, D_MODEL), F32),
        compiler_params=_cparams(("parallel",)),
        name="merge_out",
    )(oa, ob, oc, gates, h, wa, wb, wc, wo)


def _ffn_kernel(h_ref, g_ref, w1_ref, w2_ref, o_ref, hn_ref):
    @pl.when(pl.program_id(1) == 0)
    def _():
        h = h_ref[...]
        hn_ref[...] = _rmsnorm(h, g_ref[...]).astype(BF16)
        o_ref[...] = h

    a = jnp.dot(hn_ref[...], w1_ref[...], preferred_element_type=F32)
    a = jnp.square(jnp.maximum(a, 0.0))
    o_ref[...] += jnp.dot(a.astype(BF16), w2_ref[...], preferred_element_type=F32)


def _ffn(h, g, w1, w2, tf=1024):
    m, d = h.shape
    f = w1.shape[1]
    tm = _row_tile(m)
    return pl.pallas_call(
        _ffn_kernel,
        grid=(m // tm, f // tf),
        in_specs=[pl.BlockSpec((tm, d), lambda i, j: (i, 0)),
                  pl.BlockSpec((1, d), lambda i, j: (0, 0)),
                  pl.BlockSpec((d, tf), lambda i, j: (0, j)),
                  pl.BlockSpec((tf, d), lambda i, j: (j, 0))],
        out_specs=pl.BlockSpec((tm, d), lambda i, j: (i, 0)),
        out_shape=jax.ShapeDtypeStruct((m, d), F32),
        scratch_shapes=[pltpu.VMEM((tm, d), BF16)],
        compiler_params=_cparams(("parallel", "arbitrary")),
        name="ffn",
    )(h, g.reshape(1, d), w1, w2)


def _ple_kernel(h_ref, p_ref, g_ref, gf_ref, wg_ref, wp_ref, o_ref, y_ref):
    h = h_ref[...]
    pg = jax.nn.sigmoid(jnp.dot(_rmsnorm(h, g_ref[...]).astype(BF16), wg_ref[...],
                                preferred_element_type=F32))
    e = jnp.dot(p_ref[...].astype(BF16), wp_ref[...], preferred_element_type=F32)
    out = h + e * pg
    o_ref[...] = out
    y_ref[...] = _rmsnorm(out, gf_ref[...])


def _ple(h, ple, g, g_final, wg, wp):
    m, d = h.shape
    tm = 256
    row = lambda w: pl.BlockSpec((tm, w), lambda i: (i, 0))
    full = lambda shape: pl.BlockSpec(shape, lambda i: (0, 0))
    return pl.pallas_call(
        _ple_kernel,
        grid=(m // tm,),
        in_specs=[row(d), row(ple.shape[1]), full((1, d)), full((1, d)), full(wg.shape), full(wp.shape)],
        out_specs=[row(d), row(d)],
        out_shape=[jax.ShapeDtypeStruct((m, d), F32), jax.ShapeDtypeStruct((m, d), F32)],
        compiler_params=_cparams(("parallel",)),
        name="ple",
    )(h, ple, g.reshape(1, d), g_final.reshape(1, d), wg, wp)


def _sortable_key(s):
    b = lax.bitcast_convert_type(s, I32)
    return b ^ ((b >> 31) & 0x7FFFFFFF)


def _lane_sum(acc):
    ones = jnp.ones((LANES, LANES), BF16)
    return jnp.dot(acc.astype(F32).astype(BF16), ones, preferred_element_type=F32).astype(I32)


def _kth_largest_key(count_ge, rows, k):
    zero = jnp.zeros((rows, LANES), I32)
    c0 = count_ge(zero)
    thr = jnp.where(c0 >= k, zero, INT_MIN)
    cnt = jnp.where(c0 >= k, c0, k + 1)

    def pending(cnt):
        return jnp.max(jnp.where(cnt != k, 1, 0))

    def bit_step(it, thr, cnt):
        cand = thr + jnp.left_shift(jnp.int32(1), 30 - it)
        c = count_ge(cand)
        take = c >= k
        return jnp.where(take, cand, thr), jnp.where(take, c, cnt)

    thr, cnt = lax.fori_loop(0, SEARCH_FIXED_BITS, lambda it, s: bit_step(it, *s), (thr, cnt))

    def checked_step(state):
        it, thr, cnt, _ = state
        thr, cnt = bit_step(it, thr, cnt)
        return it + 1, thr, cnt, pending(cnt)

    state = lax.while_loop(lambda s: jnp.logical_and(s[0] < 31, s[3] > 0), checked_step,
                           (jnp.int32(SEARCH_FIXED_BITS), thr, cnt, pending(cnt)))
    return state[1]


def _t5_bucket(rel):
    n = jnp.maximum(rel, 0)
    max_exact = N_BUCKETS // 2
    nf = jnp.maximum(n, 1).astype(F32)
    large = max_exact + (jnp.log(nf / max_exact) / math.log(MAX_DISTANCE / max_exact)
                         * (N_BUCKETS - max_exact)).astype(I32)
    large = jnp.minimum(large, N_BUCKETS - 1)
    return jnp.where(n < max_exact, n, large)


def _bias_of_rel(rel_bias, rel):
    b = jnp.moveaxis(rel_bias[_t5_bucket(rel)], -1, 0).astype(F32)
    return jnp.where((rel >= 0)[None], b, NEG)


def _dsa_prompt_kernel(qb_ref, kb_ref, tid_ref, safe_ref, q_ref, k_ref, vt_ref, qi_ref, wi_ref, ki_ref,
                       d_ref, o_ref, keys_ref, thr_ref, m_ref, acc_ref, lg_ref, *, topk, ratio):
    p = pl.program_id(0)
    i = qb_ref[p]
    j = kb_ref[p]
    tq, tk = DSA_TQ, DSA_TK
    jd = i // ratio

    @pl.when(j == 0)
    def _select():
        w = wi_ref[...]
        qpos = i * tq + lax.broadcasted_iota(I32, (tq, tk), 0)
        lane = lax.broadcasted_iota(I32, (tq, tk), 1)

        def chunk(c, carry):
            off = pl.multiple_of(c * tk, tk)
            kic = ki_ref[pl.ds(off, tk), :]
            s = jnp.zeros((tq, tk), F32)
            for h in range(IDX_HEADS):
                s = s + jnp.maximum(_dot_nt(qi_ref[h], kic), 0.0) * w[:, h:h + 1]
            s = jnp.where(off + lane <= qpos, s, NEG)
            keys_ref[c] = _sortable_key(s)
            return carry

        lax.fori_loop(0, jd + 1, chunk, 0)

        def count_ge(cand):
            accs = []
            for r0 in range(0, tq, COUNT_ROWS):
                rs = slice(r0, r0 + COUNT_ROWS)
                cand_rows = cand[rs]

                def body(c, acc, rs=rs, cand_rows=cand_rows):
                    for t in range(tk // LANES):
                        ge = keys_ref[c, rs, t * LANES:(t + 1) * LANES] >= cand_rows
                        acc = acc + ge.astype(I32)
                    return acc

                accs.append(lax.fori_loop(0, jd + 1, body, jnp.zeros((COUNT_ROWS, LANES), I32)))
            return _lane_sum(jnp.concatenate(accs, axis=0))

        thr_ref[...] = _kth_largest_key(count_ge, tq, topk)
        m_ref[...] = jnp.full(m_ref.shape, NEG, F32)
        acc_ref[...] = jnp.zeros(acc_ref.shape, F32)

    thr = jnp.concatenate([thr_ref[...]] * (tk // LANES), axis=1)
    selb = jnp.where(keys_ref[j] >= thr, 0.0, NEG)
    bounded = safe_ref[0] == 1

    @pl.when(bounded)
    def _fixed_offset():
        for h in range(A_HEADS):
            pr = jnp.exp2(_dot_nt(q_ref[h], k_ref[h]) + (selb + d_ref[0, h])).astype(BF16)
            acc_ref[h] += _dot_nt(vt_ref[h], pr)

    @pl.when(jnp.logical_not(bounded))
    def _running_max():
        block_max = []
        for h in range(A_HEADS):
            lg = _dot_nt(q_ref[h], k_ref[h]) + (selb + d_ref[0, h])
            lg_ref[h] = lg
            block_max.append(jnp.max(lg, axis=1, keepdims=True))
        for h in range(A_HEADS):
            m_prev = m_ref[h]
            m_new = jnp.maximum(m_prev, block_max[h])
            pr = jnp.exp2(lg_ref[h] - m_new).astype(BF16)
            alpha = jnp.broadcast_to(jnp.exp2(m_prev - m_new), (tq, LANES)).T[0:VT_ROWS]
            acc_ref[h] = alpha * acc_ref[h] + _dot_nt(vt_ref[h], pr)
            m_ref[h] = m_new

    @pl.when(j == jd)
    def _finish():
        for h in range(A_HEADS):
            acc = acc_ref[h]
            out = acc[0:A_HEAD_DIM] / acc[A_HEAD_DIM:A_HEAD_DIM + 1]
            out = jnp.concatenate([out, jnp.zeros((LANES - A_HEAD_DIM, tq), F32)], axis=0).T
            o_ref[h] = out[:, 0:A_HEAD_DIM]


def _toeplitz_bias(rel_bias, base, rows, cols):
    n = rows + cols - 2
    rel = jnp.concatenate([base - jnp.arange(cols, dtype=I32),
                           base + rows - 1 - jnp.arange(rows - 1, dtype=I32)])
    z = _bias_of_rel(rel_bias, rel)
    m = jnp.tile(z, (1, rows))[:, :rows * n].reshape(z.shape[0], rows, n)
    return m[:, :, :cols]


def _prompt_bias_tiles(rel_bias):
    tq, tk = DSA_TQ, DSA_TK
    ratio = tk // tq
    tiles = [_toeplitz_bias(rel_bias, a * tq, tq, tk) for a in range(ratio)]
    tiles.append(_toeplitz_bias(rel_bias, tk, tq, tk))
    far = _bias_of_rel(rel_bias, jnp.full((1, 1), MAX_DISTANCE, I32))
    tiles.append(jnp.broadcast_to(far, (far.shape[0], tq, tk)))
    return jnp.stack(tiles) * LOG2E, jnp.max(jnp.abs(rel_bias)) * LOG2E


def _dsa_prompt(q, k, v, qi, wi, ki, bias):
    dtiles, bias_absmax = bias
    t = q.shape[0]
    tq, tk = DSA_TQ, DSA_TK
    ratio = tk // tq
    topk = min(TOPK_MAX, t // 4)
    n_q = t // tq
    n_kc = t // tk
    assert t % tk == 0 and t // LANES <= 256

    qb, kb, tid = [], [], []
    for i in range(n_q):
        jd, a = divmod(i, ratio)
        for j in range(jd + 1):
            qb.append(i)
            kb.append(j)
            tid.append(a if j == jd else (ratio if (j == jd - 1 and a == 0) else ratio + 1))
    qb, kb, tid = (jnp.asarray(np.asarray(x, np.int32)) for x in (qb, kb, tid))

    def heads(x):
        return x.reshape(t, -1, A_HEAD_DIM).transpose(1, 0, 2).astype(BF16)

    qh, kh = heads(q * (A_HEAD_DIM ** -0.5 * LOG2E)), heads(k)
    vt = v.reshape(t, A_HEADS, A_HEAD_DIM).transpose(1, 2, 0).astype(BF16)
    vt = jnp.concatenate([vt, jnp.ones((A_HEADS, 1, t), BF16),
                          jnp.zeros((A_HEADS, VT_ROWS - A_HEAD_DIM - 1, t), BF16)], axis=1)
    row_norm = lambda x: jnp.sqrt(jnp.max(jnp.sum(jnp.square(x.astype(F32)), axis=-1), axis=-1))
    bound = jnp.max(row_norm(qh) * row_norm(kh)) + bias_absmax
    safe = (bound <= LOGIT_BOUND).astype(I32).reshape(1)

    grid_spec = pltpu.PrefetchScalarGridSpec(
        num_scalar_prefetch=4,
        grid=(int(qb.shape[0]),),
        in_specs=[
            pl.BlockSpec((A_HEADS, tq, A_HEAD_DIM), lambda p, qb, kb, tid, sf: (0, qb[p], 0)),
            pl.BlockSpec((A_HEADS, tk, A_HEAD_DIM), lambda p, qb, kb, tid, sf: (0, kb[p], 0)),
            pl.BlockSpec((A_HEADS, VT_ROWS, tk), lambda p, qb, kb, tid, sf: (0, 0, kb[p])),
            pl.BlockSpec((IDX_HEADS, tq, IDX_DIM), lambda p, qb, kb, tid, sf: (0, qb[p], 0)),
            pl.BlockSpec((tq, IDX_HEADS), lambda p, qb, kb, tid, sf: (qb[p], 0)),
            pl.BlockSpec((t, IDX_DIM), lambda p, qb, kb, tid, sf: (0, 0)),
            pl.BlockSpec((1, A_HEADS, tq, tk), lambda p, qb, kb, tid, sf: (tid[p], 0, 0, 0)),
        ],
        out_specs=pl.BlockSpec((A_HEADS, tq, A_HEAD_DIM), lambda p, qb, kb, tid, sf: (0, qb[p], 0)),
        scratch_shapes=[pltpu.VMEM((n_kc, tq, tk), I32),
                        pltpu.VMEM((tq, LANES), I32),
                        pltpu.VMEM((A_HEADS, tq, 1), F32),
                        pltpu.VMEM((A_HEADS, VT_ROWS, tq), F32),
                        pltpu.VMEM((A_HEADS, tq, tk), F32)],
    )
    out = pl.pallas_call(
        functools.partial(_dsa_prompt_kernel, topk=topk, ratio=ratio),
        grid_spec=grid_spec,
        out_shape=jax.ShapeDtypeStruct((A_HEADS, t, A_HEAD_DIM), F32),
        compiler_params=_cparams(("arbitrary",)),
        name="dsa_prompt",
    )(qb, kb, tid, safe, qh, kh, vt, heads(qi), wi, ki.astype(BF16), dtiles)
    return out.transpose(1, 0, 2).reshape(t, A_WIDTH)


def _dsa_sample_select_kernel(pt_ref, qi_ref, wi_ref, *rest, n_pages, pps, n_new, topk):
    cki_refs = rest[:pps]
    kin_ref, keys_ref, thr_ref = rest[pps:]
    p = pl.program_id(1)
    rows = n_new
    page = kin_ref.shape[2]
    w = wi_ref[0][:, 0:1]

    def scores(kt):
        r = jnp.maximum(jnp.dot(qi_ref[0], kt.astype(BF16), preferred_element_type=F32), 0.0) * w
        return jnp.sum(r.reshape(IDX_HEADS, rows, r.shape[-1]), axis=0)

    s = scores(jnp.concatenate([r[0, 0] for r in cki_refs], axis=1))
    for r in range(pps):
        keys_ref[0, p * pps + r] = _sortable_key(s[:, r * page:(r + 1) * page])

    @pl.when(p == n_pages // pps - 1)
    def _new():
        s = scores(kin_ref[0])
        qrow = lax.broadcasted_iota(I32, s.shape, 0)
        col = lax.broadcasted_iota(I32, s.shape, 1)
        keys_ref[0, n_pages] = _sortable_key(jnp.where(col <= qrow, s, NEG))

        def count_ge(cand):
            parts = [(keys_ref[0, c] >= cand).astype(I32) for c in range(n_pages + 1)]
            while len(parts) > 1:
                parts = [a + b for a, b in zip(parts[::2], parts[1::2])] + parts[len(parts) & ~1:]
            return _lane_sum(parts[0])

        thr_ref[0] = _kth_largest_key(count_ge, rows, topk)


def _dsa_sample_attend_kernel(pt_ref, q_ref, keys_ref, thr_ref, *rest, n_pages, pps):
    ck_refs, cv_refs = rest[:pps], rest[pps:2 * pps]
    (kn_ref, vn_ref, dfar_ref, dlast_ref, dnew_ref, o_ref, m_ref, l_ref, acc_ref) = rest[2 * pps:]
    p = pl.program_id(1)
    last = p == n_pages // pps - 1
    rows = keys_ref.shape[2]
    hd = A_HEADS * A_HEAD_DIM

    @pl.when(p == 0)
    def _init():
        m_ref[...] = jnp.full(m_ref.shape, NEG, F32)
        l_ref[...] = jnp.zeros(l_ref.shape, F32)
        acc_ref[...] = jnp.zeros(acc_ref.shape, F32)

    thr = thr_ref[0]

    def attend(kt, vt, key_pages, dtile):
        selb = jnp.concatenate([jnp.where(kp >= thr, 0.0, NEG) for kp in key_pages], axis=1)
        selb = jnp.concatenate([selb] * A_HEADS, axis=0)
        lg = jnp.dot(q_ref[0], kt, preferred_element_type=F32) + (selb + dtile)
        m_prev = m_ref[...]
        m_new = jnp.maximum(m_prev, jnp.max(lg, axis=1, keepdims=True))
        alpha = jnp.exp2(m_prev - m_new)
        pr = jnp.exp2(lg - m_new)
        l_ref[...] = alpha * l_ref[...] + jnp.sum(pr, axis=1, keepdims=True)
        acc_ref[...] = alpha * acc_ref[...] + _dot_nt(pr.astype(BF16), vt)
        m_ref[...] = m_new

    kt = jnp.concatenate([r[0, 0].reshape(hd, -1).astype(BF16) for r in ck_refs], axis=1)
    vt = jnp.concatenate([r[0, 0].reshape(hd, -1).astype(BF16) for r in cv_refs], axis=1)
    attend(kt, vt, [keys_ref[0, p * pps + r] for r in range(pps)],
           jnp.where(last, dlast_ref[...], dfar_ref[...]))

    @pl.when(last)
    def _new():
        attend(kn_ref[0].astype(BF16), vn_ref[0].astype(BF16), [keys_ref[0, n_pages]], dnew_ref[...])
        acc = acc_ref[...]
        l = l_ref[...]
        for h in range(A_HEADS):
            rs = slice(h * rows, (h + 1) * rows)
            o_ref[0, rs, :] = acc[rs, h * A_HEAD_DIM:(h + 1) * A_HEAD_DIM] / l[rs]


def _sample_bias_tiles(rel_bias, n_new, page):
    pps = SAMPLE_PAGES_PER_STEP
    qq = jnp.arange(n_new, dtype=I32)[:, None]
    pos = jnp.arange(page, dtype=I32)[None, :]
    tile = lambda rel: _bias_of_rel(rel_bias, rel).reshape(A_HEADS * n_new, page) * LOG2E
    far = tile(jnp.full((n_new, page), MAX_DISTANCE, I32))
    last = tile(page + qq - pos)
    new = tile(qq - pos)
    return jnp.tile(far, (1, pps)), jnp.concatenate([far] * (pps - 1) + [last], axis=1), new


def _dsa_sample(layer, q, k_new, v_new, qi, wi, ki_new, cache_k, cache_v, cache_idx_k,
                page_table, dtiles):
    bd, n_pages = page_table.shape
    page = cache_k.shape[2]
    n_new = q.shape[0] // bd
    past = n_pages * page
    topk = min(TOPK_MAX, (past + n_new) // 4)
    hrows = A_HEADS * n_new
    pcols = page * A_HEADS
    pps = SAMPLE_PAGES_PER_STEP
    n_steps = n_pages // pps
    assert n_pages % pps == 0 and n_pages + 1 <= 256
    pt = page_table.reshape(-1).astype(I32)

    def head_rows(x):
        return x.reshape(bd, n_new, -1, A_HEAD_DIM).transpose(0, 2, 1, 3).reshape(bd, hrows, A_HEAD_DIM)

    qi2 = head_rows(qi).astype(BF16)
    wi2 = jnp.broadcast_to(wi.reshape(bd, n_new, IDX_HEADS).transpose(0, 2, 1).reshape(bd, hrows, 1),
                           (bd, hrows, LANES))

    def lanes_last(x):
        x = x.reshape(bd, n_new, -1).transpose(0, 2, 1)
        return jnp.pad(x, ((0, 0), (0, 0), (0, page - n_new)))

    cki_t = cache_idx_k.transpose(0, 1, 3, 2)
    ck_t = cache_k.transpose(0, 1, 3, 4, 2)
    cv_t = cache_v.transpose(0, 1, 3, 4, 2)

    def page_spec(block, r, pps=pps):
        zeros = (0,) * (len(block) - 2)
        return pl.BlockSpec(block, lambda b, p, pt: (layer, pt[b * n_pages + p * pps + r]) + zeros)

    per_batch = lambda block: pl.BlockSpec(block, lambda b, p, pt: (b,) + (0,) * (len(block) - 1))
    const = lambda block: pl.BlockSpec(block, lambda b, p, pt: (0,) * len(block))

    spp = 2 * pps if n_pages % (2 * pps) == 0 else pps
    keys, thr = pl.pallas_call(
        functools.partial(_dsa_sample_select_kernel, n_pages=n_pages, pps=spp, n_new=n_new, topk=topk),
        grid_spec=pltpu.PrefetchScalarGridSpec(
            num_scalar_prefetch=1,
            grid=(bd, n_pages // spp),
            in_specs=([per_batch((1, hrows, IDX_DIM)), per_batch((1, hrows, LANES))]
                      + [page_spec((1, 1, IDX_DIM, page), r, spp) for r in range(spp)]
                      + [per_batch((1, IDX_DIM, page))]),
            out_specs=[per_batch((1, n_pages + 1, n_new, page)), per_batch((1, n_new, LANES))],
        ),
        out_shape=[jax.ShapeDtypeStruct((bd, n_pages + 1, n_new, page), I32),
                   jax.ShapeDtypeStruct((bd, n_new, LANES), I32)],
        compiler_params=_cparams(("parallel", "arbitrary")),
        name="dsa_sample_select",
    )(pt, qi2, wi2, *([cki_t] * spp), lanes_last(ki_new))

    dfar, dlast, dnew = dtiles
    q4 = (q * (A_HEAD_DIM ** -0.5 * LOG2E)).reshape(bd, n_new, A_HEADS, A_HEAD_DIM)
    qbd = jnp.einsum("bqhd,hg->bhqgd", q4, jnp.eye(A_HEADS, dtype=F32)).reshape(bd, hrows, A_WIDTH)

    cache_block = (1, 1, A_HEADS, A_HEAD_DIM, page)
    out = pl.pallas_call(
        functools.partial(_dsa_sample_attend_kernel, n_pages=n_pages, pps=pps),
        grid_spec=pltpu.PrefetchScalarGridSpec(
            num_scalar_prefetch=1,
            grid=(bd, n_steps),
            in_specs=([per_batch((1, hrows, A_WIDTH)), per_batch((1, n_pages + 1, n_new, page)),
                       per_batch((1, n_new, LANES))]
                      + [page_spec(cache_block, r) for r in range(pps)]
                      + [page_spec(cache_block, r) for r in range(pps)]
                      + [per_batch((1, A_WIDTH, page)), per_batch((1, A_WIDTH, page)),
                         const((hrows, pps * page)), const((hrows, pps * page)), const((hrows, page))]),
            out_specs=per_batch((1, hrows, A_HEAD_DIM)),
            scratch_shapes=[pltpu.VMEM((hrows, 1), F32), pltpu.VMEM((hrows, 1), F32),
                            pltpu.VMEM((hrows, A_WIDTH), F32)],
        ),
        out_shape=jax.ShapeDtypeStruct((bd, hrows, A_HEAD_DIM), F32),
        compiler_params=_cparams(("parallel", "arbitrary")),
        name="dsa_sample_attend",
    )(pt, qbd.astype(BF16), keys, thr, *([ck_t] * pps), *([cv_t] * pps),
      lanes_last(k_new), lanes_last(v_new), dfar, dlast, dnew)
    return out.reshape(bd, A_HEADS, n_new, A_HEAD_DIM).transpose(0, 2, 1, 3).reshape(bd * n_new, A_WIDTH)


def _s5_in_kernel(u_ref, bb_ref, o_ref):
    o_ref[...] = jnp.dot(u_ref[...].astype(BF16), bb_ref[...], preferred_element_type=F32)


def _s5_scan_kernel(bu_ref, s0_ref, ab_ref, o_ref, fin_ref, st_ref, *, tc):
    @pl.when(pl.program_id(1) == 0)
    def _():
        st_ref[...] = s0_ref[0]

    ar, ai = ab_ref[0], ab_ref[1]

    def step(t, carry):
        sr, si = carry
        nr = ar * sr - ai * si + bu_ref[0, t, 0]
        ni = ar * si + ai * sr + bu_ref[0, t, 1]
        o_ref[0, t, 0] = nr
        o_ref[0, t, 1] = ni
        return nr, ni

    sr, si = lax.fori_loop(0, tc, step, (st_ref[0], st_ref[1]), unroll=8)
    st_ref[0] = sr
    st_ref[1] = si
    fin_ref[0, 0] = sr
    fin_ref[0, 1] = si


def _s5_out_kernel(s_ref, u_ref, c_ref, d_ref, wg_ref, o_ref):
    u = u_ref[...]
    y = jnp.dot(s_ref[...].astype(BF16), c_ref[...], preferred_element_type=F32) + u * d_ref[...]
    hdn = jax.nn.gelu(y)
    gate = jax.nn.sigmoid(jnp.dot(hdn.astype(BF16), wg_ref[...], preferred_element_type=F32))
    o_ref[...] = hdn * gate


def _s5(u, s0_re, s0_im, lam_re, lam_im, log_step, b_re, b_im, c_re, c_im, d, w_glu, batch):
    m = u.shape[0]
    t = m // batch
    g, p = S5_GROUPS, S5_STATE
    n_state = g * p
    lam_re = jnp.minimum(lam_re.astype(F32), -1e-4)
    lam_im = lam_im.astype(F32)
    dt = jnp.exp(log_step.astype(F32))[:, None]
    mag = jnp.exp(lam_re * dt)
    ab_re, ab_im = mag * jnp.cos(lam_im * dt), mag * jnp.sin(lam_im * dt)
    den = lam_re * lam_re + lam_im * lam_im
    nr, ni = ab_re - 1.0, ab_im
    f_re = (nr * lam_re + ni * lam_im) / den
    f_im = (ni * lam_re - nr * lam_im) / den
    bb_re = f_re[..., None] * b_re - f_im[..., None] * b_im
    bb_im = f_re[..., None] * b_im + f_im[..., None] * b_re
    eye = jnp.eye(g, dtype=F32)
    blk_in = lambda x: jnp.einsum("gph,gk->ghkp", x, eye).reshape(S5_WIDTH, n_state)
    blk_out = lambda x: jnp.einsum("ghp,gk->kpgh", x, eye).reshape(n_state, S5_WIDTH)
    bb = jnp.concatenate([blk_in(bb_re), blk_in(bb_im)], axis=1).astype(BF16)
    cc = jnp.concatenate([blk_out(c_re.astype(F32)), -blk_out(c_im.astype(F32))], axis=0).astype(BF16)
    ab = jnp.stack([ab_re.reshape(SUBLANES, LANES), ab_im.reshape(SUBLANES, LANES)])
    s0 = jnp.stack([s0_re.reshape(batch, SUBLANES, LANES), s0_im.reshape(batch, SUBLANES, LANES)], axis=1)

    tm = 256
    bu = pl.pallas_call(
        _s5_in_kernel,
        grid=(m // tm,),
        in_specs=[pl.BlockSpec((tm, S5_WIDTH), lambda i: (i, 0)),
                  pl.BlockSpec(bb.shape, lambda i: (0, 0))],
        out_specs=pl.BlockSpec((tm, 2 * n_state), lambda i: (i, 0)),
        out_shape=jax.ShapeDtypeStruct((m, 2 * n_state), F32),
        compiler_params=_cparams(("parallel",)),
        name="s5_in",
    )(u, bb)

    tc = min(t, 256)
    tile = (1, tc, 2, SUBLANES, LANES)
    states, fin = pl.pallas_call(
        functools.partial(_s5_scan_kernel, tc=tc),
        grid=(batch, t // tc),
        in_specs=[pl.BlockSpec(tile, lambda b, c: (b, c, 0, 0, 0)),
                  pl.BlockSpec((1, 2, SUBLANES, LANES), lambda b, c: (b, 0, 0, 0)),
                  pl.BlockSpec((2, SUBLANES, LANES), lambda b, c: (0, 0, 0))],
        out_specs=[pl.BlockSpec(tile, lambda b, c: (b, c, 0, 0, 0)),
                   pl.BlockSpec((1, 2, SUBLANES, LANES), lambda b, c: (b, 0, 0, 0))],
        out_shape=[jax.ShapeDtypeStruct((batch, t, 2, SUBLANES, LANES), F32),
                   jax.ShapeDtypeStruct((batch, 2, SUBLANES, LANES), F32)],
        scratch_shapes=[pltpu.VMEM((2, SUBLANES, LANES), F32)],
        compiler_params=_cparams(("parallel", "arbitrary")),
        name="s5_scan",
    )(bu.reshape(batch, t, 2, SUBLANES, LANES), s0, ab)

    out = pl.pallas_call(
        _s5_out_kernel,
        grid=(m // tm,),
        in_specs=[pl.BlockSpec((tm, 2 * n_state), lambda i: (i, 0)),
                  pl.BlockSpec((tm, S5_WIDTH), lambda i: (i, 0)),
                  pl.BlockSpec(cc.shape, lambda i: (0, 0)),
                  pl.BlockSpec((1, S5_WIDTH), lambda i: (0, 0)),
                  pl.BlockSpec((S5_WIDTH, S5_WIDTH), lambda i: (0, 0))],
        out_specs=pl.BlockSpec((tm, S5_WIDTH), lambda i: (i, 0)),
        out_shape=jax.ShapeDtypeStruct((m, S5_WIDTH), F32),
        compiler_params=_cparams(("parallel",)),
        name="s5_out",
    )(states.reshape(m, 2 * n_state), u, cc, d.reshape(1, S5_WIDTH).astype(F32), w_glu.astype(BF16))
    return out, fin[:, 0].reshape(batch, g, p), fin[:, 1].reshape(batch, g, p)


_RW_OPS = ("r", "k", "v", "d", "kk", "ka")


def _head_sum_matrix():
    hid = np.arange(RW_WIDTH) // RW_HEAD_DIM
    return jnp.asarray((hid[:, None] == hid[None, :]).astype(np.float32))


def _rwkv_pre_kernel(*refs, with_vres):
    if with_vres:
        (pc_ref, prev_ref, vf_ref, mu_ref, w0_ref, w2_ref, a0_ref, a2_ref, g2_ref, kk_ref, ka_ref,
         rk_ref, j_ref, v0_ref, v1_ref, v2_ref, ops_ref, gb_ref) = refs
    else:
        (pc_ref, prev_ref, mu_ref, w0_ref, w2_ref, a0_ref, a2_ref, g2_ref, kk_ref, ka_ref,
         rk_ref, j_ref, ops_ref, gb_ref) = refs
    pc = pc_ref[...]
    xm = pc + (prev_ref[...] - pc) * mu_ref[...]
    w_ = RW_WIDTH
    r, k, v = xm[:, 0:w_], xm[:, w_:2 * w_], xm[:, 2 * w_:3 * w_]
    o = 3 * w_
    lora = xm[:, o:o + RW_DECAY_LORA + RW_AAA_LORA]
    gl = xm[:, o + RW_DECAY_LORA + RW_AAA_LORA:]
    mm = lambda a, b_ref: jnp.dot(a.astype(BF16), b_ref[...].astype(BF16), preferred_element_type=F32)
    z = -(w0_ref[...] + mm(jnp.tanh(lora), w2_ref))
    softplus = jnp.maximum(z, 0.0) + jnp.log(1.0 + jnp.exp(-jnp.abs(z)))
    w = -softplus - 0.5
    decay = jnp.exp(-jnp.exp(w))
    a = jax.nn.sigmoid(a0_ref[...] + mm(lora, a2_ref))
    g = mm(jax.nn.sigmoid(gl), g2_ref)
    if with_vres:
        v = v + (vf_ref[...] - v) * jax.nn.sigmoid(v0_ref[...] + mm(mm(v, v1_ref), v2_ref))
    jm = j_ref[...]
    kk = k * kk_ref[...]
    kk = kk * lax.rsqrt(jnp.maximum(_dot_exact(kk * kk, jm), 1e-24))
    k = k * (1.0 + (a - 1.0) * ka_ref[...])
    bonus = _dot_exact(r * k * rk_ref[...], jm) * v
    for idx, val in enumerate((r, k, v, 1.0 - decay, kk, kk * a)):
        ops_ref[:, idx * w_:(idx + 1) * w_] = val
    gb_ref[:, 0:w_] = g
    gb_ref[:, w_:2 * w_] = bonus


def _rwkv_scan_kernel(ops_ref, s0_ref, mask_ref, j_ref, y_ref, fin_ref, st_ref, zh_ref, e_ref,
                      *, tc, sub):
    @pl.when(pl.program_id(1) == 0)
    def _():
        st_ref[...] = s0_ref[0]

    n, w_ = RW_HEAD_DIM, RW_WIDTH
    col = {name: idx * w_ for idx, name in enumerate(_RW_OPS)}
    expand = ("d", "kk", "ka", "k", "r")
    mask = mask_ref[...] > 0.5
    jm = j_ref[...]

    def sub_chunk(sc, state):
        t0 = pl.multiple_of(sc * sub, sub)
        for oi, name in enumerate(expand):
            for t in range(sub):
                row = ops_ref[0, pl.ds(t0 + t, 1), col[name]:col[name] + w_]
                zh_ref[t * n:(t + 1) * n, :] = jnp.where(mask, jnp.broadcast_to(row, (n, w_)), 0.0).astype(BF16)
            e_ref[oi] = jnp.dot(zh_ref[...], jm, preferred_element_type=F32)
        for t in range(sub):
            rows = slice(t * n, (t + 1) * n)
            vrow = ops_ref[0, pl.ds(t0 + t, 1), col["v"]:col["v"] + w_]
            sa = -jnp.sum(state * e_ref[1, rows, :], axis=0, keepdims=True)
            state = (state - e_ref[0, rows, :] * state) + e_ref[2, rows, :] * sa + e_ref[3, rows, :] * vrow
            y_ref[0, pl.ds(t0 + t, 1), :] = jnp.sum(state * e_ref[4, rows, :], axis=0, keepdims=True)
        return state

    state = lax.fori_loop(0, tc // sub, sub_chunk, st_ref[...])
    st_ref[...] = state
    fin_ref[0] = state


def _rwkv_post_kernel(y_ref, gb_ref, lw_ref, lb_ref, j_ref, o_ref):
    y = y_ref[...]
    jm = j_ref[...]
    inv = 1.0 / RW_HEAD_DIM
    mean = _dot_exact(y, jm) * inv
    yc = y - mean
    var = _dot_exact(yc * yc, jm) * inv
    yn = yc * lax.rsqrt(var + RW_GN_EPS) * lw_ref[...] + lb_ref[...]
    gb = gb_ref[...]
    o_ref[...] = (yn + gb[:, RW_WIDTH:]) * gb[:, 0:RW_WIDTH]


def _rwkv(pc, shift_prev, wkv0, v_first, vres, mu, w0, w2, a0, a2, g2, k_k, k_a, r_k, ln_w, ln_b, batch):
    m = pc.shape[0]
    t = m // batch
    w_ = RW_WIDTH
    pc3 = pc.reshape(batch, t, RW_PROJ)
    prev = jnp.concatenate([shift_prev.astype(F32)[:, None], pc3[:, :-1]], axis=1).reshape(m, RW_PROJ)
    jm = _head_sum_matrix()
    row = lambda x: x.reshape(1, -1).astype(F32)
    tm = 256
    rspec = lambda width: pl.BlockSpec((tm, width), lambda i: (i, 0))
    fspec = lambda a: pl.BlockSpec(a.shape, lambda i: (0,) * a.ndim)
    w2p = jnp.pad(w2, ((0, RW_AAA_LORA), (0, 0)))
    a2p = jnp.pad(a2, ((RW_DECAY_LORA, 0), (0, 0)))
    params = [row(mu), row(w0), w2p, row(a0), a2p, g2, row(k_k), row(k_a), row(r_k), jm]
    ins = [pc, prev]
    specs = [rspec(RW_PROJ), rspec(RW_PROJ)]
    if vres is not None:
        ins.append(v_first)
        specs.append(rspec(w_))
        params += [row(vres[0]), vres[1], vres[2]]
    ops, gb = pl.pallas_call(
        functools.partial(_rwkv_pre_kernel, with_vres=vres is not None),
        grid=(m // tm,),
        in_specs=specs + [fspec(a) for a in params],
        out_specs=[rspec(len(_RW_OPS) * w_), rspec(2 * w_)],
        out_shape=[jax.ShapeDtypeStruct((m, len(_RW_OPS) * w_), F32),
                   jax.ShapeDtypeStruct((m, 2 * w_), F32)],
        compiler_params=_cparams(("parallel",)),
        name="rwkv_pre",
    )(*ins, *params)
    if vres is None:
        v_first = ops[:, 2 * w_:3 * w_]

    n = RW_HEAD_DIM
    sub = 8
    tc = min(t, 64)
    jidx = np.arange(w_) % n
    mask = jnp.asarray((np.arange(n)[:, None] == jidx[None, :]).astype(np.float32))
    p0 = wkv0.astype(F32).transpose(0, 3, 1, 2).reshape(batch, n, w_)
    y, fin = pl.pallas_call(
        functools.partial(_rwkv_scan_kernel, tc=tc, sub=sub),
        grid=(batch, t // tc),
        in_specs=[pl.BlockSpec((1, tc, len(_RW_OPS) * w_), lambda b, c: (b, c, 0)),
                  pl.BlockSpec((1, n, w_), lambda b, c: (b, 0, 0)),
                  pl.BlockSpec((n, w_), lambda b, c: (0, 0)),
                  pl.BlockSpec((w_, w_), lambda b, c: (0, 0))],
        out_specs=[pl.BlockSpec((1, tc, w_), lambda b, c: (b, c, 0)),
                   pl.BlockSpec((1, n, w_), lambda b, c: (b, 0, 0))],
        out_shape=[jax.ShapeDtypeStruct((batch, t, w_), F32),
                   jax.ShapeDtypeStruct((batch, n, w_), F32)],
        scratch_shapes=[pltpu.VMEM((n, w_), F32),
                        pltpu.VMEM((sub * n, w_), BF16),
                        pltpu.VMEM((5, sub * n, w_), F32)],
        compiler_params=_cparams(("parallel", "arbitrary")),
        name="rwkv_scan",
    )(ops.reshape(batch, t, len(_RW_OPS) * w_), p0, mask, jm.astype(BF16))
    wkv = fin.reshape(batch, n, RW_HEADS, n).transpose(0, 2, 3, 1)

    out = pl.pallas_call(
        _rwkv_post_kernel,
        grid=(m // tm,),
        in_specs=[rspec(w_), rspec(2 * w_), fspec(row(ln_w)), fspec(row(ln_b)), fspec(jm)],
        out_specs=rspec(w_),
        out_shape=jax.ShapeDtypeStruct((m, w_), F32),
        compiler_params=_cparams(("parallel",)),
        name="rwkv_post",
    )(y.reshape(m, w_), gb, row(ln_w), row(ln_b), jm)
    return out, wkv, v_first


def _relayout_w_in(w_in):
    cols = []
    for name, width in _PIECES:
        src, _, _ = _LAYOUT[name]
        piece = w_in[:, src:src + width]
        pad = -(-width // LANES) * LANES - width
        cols.append(jnp.pad(piece, ((0, 0), (0, pad))) if pad else piece)
    return jnp.concatenate(cols, axis=1).astype(BF16)


def _proj_scale_row():
    s = np.ones((1, PROJ_COLS), np.float32)
    _, d0, w0 = _LAYOUT["qi"]
    s[0, d0:d0 + w0] = IDX_DIM ** -0.5
    _, d1, w1 = _LAYOUT["wi"]
    s[0, d1:d1 + w1] = IDX_HEADS ** -0.5
    return jnp.asarray(s)


def _trunk(x, ple, attend, s5_re0, s5_im0, wkv0, shift0, W):
    batch, t, _ = x.shape
    m = batch * t
    depth = W["w_in"].shape[0]
    h = x.reshape(m, D_MODEL)
    v_first = None
    ks, vs, kis, sres, sims, wkvs, shifts = [], [], [], [], [], [], []
    scale_row = _proj_scale_row()
    ones_row = jnp.ones((1, 3 * D_MODEL), F32)
    y = None

    def piece(proj, name):
        _, dst, width = _LAYOUT[name]
        return proj[:, dst:dst + width]

    for i in range(depth):
        proj = _norm_mm(h, W["norm_mix"][i], _relayout_w_in(W["w_in"][i]), scale_row, tn=PROJ_COLS // 4)
        gates = _norm_mm(h, W["norm_mix"][i], W["w_gate"][i].astype(BF16), ones_row, act="sigmoid",
                         tn=D_MODEL)
        q, k, v, qi, wi, ki, u, pc = (piece(proj, n) for n, _ in _PIECES)
        o_a = attend(i, q, k, v, qi, wi, ki)
        o_b, s_re, s_im = _s5(u, s5_re0[i], s5_im0[i], W["ssm_lambda_re"][i], W["ssm_lambda_im"][i],
                              W["ssm_log_step"][i], W["ssm_b_re"][i], W["ssm_b_im"][i],
                              W["ssm_c_re"][i], W["ssm_c_im"][i], W["ssm_d"][i], W["ssm_w_glu"][i], batch)
        vres = None if i == 0 else (W["rw_v0"][i - 1], W["rw_v1"][i - 1], W["rw_v2"][i - 1])
        o_c, wkv, v_first = _rwkv(pc, shift0[i], wkv0[i], v_first, vres, W["rw_mu"][i], W["rw_w0"][i],
                                  W["rw_w2"][i], W["rw_a0"][i], W["rw_a2"][i], W["rw_g2"][i],
                                  W["rw_k_k"][i], W["rw_k_a"][i], W["rw_r_k"][i], W["rw_ln_w"][i],
                                  W["rw_ln_b"][i], batch)
        h = _merge(o_a, o_b, o_c, gates, h, W["w_up_a"][i].astype(BF16), W["w_up_b"][i].astype(BF16),
                   W["w_up_c"][i].astype(BF16), W["w_out"][i].astype(BF16))
        h = _ffn(h, W["norm_ffn"][i], W["w_ff1"][i].astype(BF16), W["w_ff2"][i].astype(BF16))
        h, y = _ple(h, ple[i].reshape(m, -1), W["norm_ple"][i], W["norm_final"],
                    W["w_ple_gate"][i].astype(BF16), W["w_ple"][i].astype(BF16))
        ks.append(k.reshape(batch, t, A_HEADS, A_HEAD_DIM))
        vs.append(v.reshape(batch, t, A_HEADS, A_HEAD_DIM))
        kis.append(ki.reshape(batch, t, IDX_DIM))
        sres.append(s_re)
        sims.append(s_im)
        wkvs.append(wkv)
        shifts.append(pc.reshape(batch, t, RW_PROJ)[:, -1])
    return (y.reshape(batch, t, D_MODEL), jnp.stack(ks), jnp.stack(vs), jnp.stack(kis), jnp.stack(sres),
            jnp.stack(sims), jnp.stack(wkvs), jnp.stack(shifts))


def kernel(x_prompt, x_sample, cache_k, cache_v, cache_idx_k, state_ssm_re, state_ssm_im, state_wkv,
           state_shift, page_table, p_prompt, p_sample, rel_bias, norm_mix, norm_ffn, norm_ple, norm_final,
           w_in, w_up_a, w_up_b, w_up_c, w_gate, w_out, ssm_lambda_re, ssm_lambda_im, ssm_log_step,
           ssm_b_re, ssm_b_im, ssm_c_re, ssm_c_im, ssm_d, ssm_w_glu, rw_mu, rw_w0, rw_w2, rw_a0, rw_a2,
           rw_g2, rw_k_k, rw_k_a, rw_r_k, rw_ln_w, rw_ln_b, rw_v0, rw_v1, rw_v2, w_ff1, w_ff2,
           w_ple, w_ple_gate):
    W = dict(norm_mix=norm_mix, norm_ffn=norm_ffn, norm_ple=norm_ple, norm_final=norm_final, w_in=w_in,
             w_up_a=w_up_a, w_up_b=w_up_b, w_up_c=w_up_c, w_gate=w_gate, w_out=w_out,
             ssm_lambda_re=ssm_lambda_re, ssm_lambda_im=ssm_lambda_im, ssm_log_step=ssm_log_step,
             ssm_b_re=ssm_b_re, ssm_b_im=ssm_b_im, ssm_c_re=ssm_c_re, ssm_c_im=ssm_c_im, ssm_d=ssm_d,
             ssm_w_glu=ssm_w_glu, rw_mu=rw_mu, rw_w0=rw_w0, rw_w2=rw_w2, rw_a0=rw_a0, rw_a2=rw_a2,
             rw_g2=rw_g2, rw_k_k=rw_k_k, rw_k_a=rw_k_a, rw_r_k=rw_r_k, rw_ln_w=rw_ln_w, rw_ln_b=rw_ln_b,
             rw_v0=rw_v0, rw_v1=rw_v1, rw_v2=rw_v2, w_ff1=w_ff1, w_ff2=w_ff2, w_ple=w_ple,
             w_ple_gate=w_ple_gate)
    depth = w_in.shape[0]
    bp = x_prompt.shape[0]

    prompt_tiles = _prompt_bias_tiles(rel_bias)
    sample_tiles = _sample_bias_tiles(rel_bias, x_sample.shape[1], cache_k.shape[2])

    def attend_prompt(i, q, k, v, qi, wi, ki):
        return _dsa_prompt(q, k, v, qi, wi, ki, prompt_tiles)

    def attend_sample(i, q, k, v, qi, wi, ki):
        return _dsa_sample(i, q, k, v, qi, wi, ki, cache_k, cache_v, cache_idx_k, page_table, sample_tiles)

    zs = jnp.zeros((depth, bp, S5_GROUPS, S5_STATE), F32)
    zw = jnp.zeros((depth, bp, RW_HEADS, RW_HEAD_DIM, RW_HEAD_DIM), F32)
    zsh = jnp.zeros((depth, bp, RW_PROJ), x_prompt.dtype)
    (y_prompt, k_prompt, v_prompt, idx_k_prompt, ssm_re_prompt, ssm_im_prompt, wkv_prompt,
     shift_prompt) = _trunk(x_prompt, p_prompt, attend_prompt, zs, zs, zw, zsh, W)
    (y_sample, k_sample, v_sample, idx_k_sample, ssm_re_sample, ssm_im_sample, wkv_sample,
     shift_sample) = _trunk(x_sample, p_sample, attend_sample, state_ssm_re, state_ssm_im, state_wkv,
                            state_shift, W)
    return (y_prompt, y_sample, k_prompt, v_prompt, idx_k_prompt, k_sample, v_sample, idx_k_sample,
            ssm_re_prompt, ssm_im_prompt, ssm_re_sample, ssm_im_sample, wkv_prompt, wkv_sample,
            shift_prompt, shift_sample)
```
